```python
import math
import jax, jax.numpy as jnp
from jax import lax
import numpy as np

D_MODEL = 1024
BATCH = 8
SEQ = 2048
DEPTH = 1

CHUNK = 64
GM_WIDTH = 1024
GM_BLOCK = 128
GM_GROUPS = 8
GM_GROUP_DIM = GM_WIDTH // GM_GROUPS
RW_WIDTH = 1024
RW_HEAD_DIM = 64
RW_HEADS = RW_WIDTH // RW_HEAD_DIM
RW_DECAY_RANK = 64
RW_ICLR_RANK = 64
DECAY_SCALE = math.exp(-0.5)
GM_COLS = 3 * GM_WIDTH
RW_COLS = 4 * RW_WIDTH + RW_DECAY_RANK + RW_ICLR_RANK
GATE_COLS = 2 * D_MODEL
IN_COLS = GM_COLS + RW_COLS + GATE_COLS
RMS_EPS = 1e-6
LN_EPS = 1e-5
GN_EPS = 64e-5
L2_EPS = 1e-12

kernel_name = "hybrid_gmlp_rwkv7_gated_block"


def rms_norm(x, g):
    xf = x.astype(jnp.float32)
    y = xf * lax.rsqrt(jnp.mean(xf * xf, axis=-1, keepdims=True) + RMS_EPS)
    return (y * g.astype(jnp.float32)).astype(x.dtype)


def layer_norm(x, g, b):
    xf = x.astype(jnp.float32)
    mu = jnp.mean(xf, axis=-1, keepdims=True)
    var = jnp.mean(jnp.square(xf - mu), axis=-1, keepdims=True)
    y = (xf - mu) * lax.rsqrt(var + LN_EPS) * g.astype(jnp.float32) + b.astype(jnp.float32)
    return y.astype(x.dtype)


def token_shift(p):
    return jnp.pad(p, ((0, 0), (1, 0), (0, 0)))[:, :-1]


def gmlp_branch(pa, ln_g, ln_b, w_s, b_s):
    B, S, _ = pa.shape
    u = jax.nn.gelu(pa[..., :GM_WIDTH])
    v = jax.nn.gelu(pa[..., GM_WIDTH:2 * GM_WIDTH])
    z = pa[..., 2 * GM_WIDTH:]
    v = layer_norm(v, ln_g, ln_b)
    nb = S // GM_BLOCK
    v = v.reshape(B, nb, GM_BLOCK, GM_GROUPS, GM_GROUP_DIM)
    chunk_id = jnp.arange(GM_BLOCK) // CHUNK
    mask = chunk_id[:, None] >= chunk_id[None, :]
    w = jnp.where(mask[None], w_s, 0.0)
    s = jnp.einsum('gij,bnjgc->bnigc', w, v) + b_s.T[None, None, :, :, None]
    s = s.reshape(B, S, GM_WIDTH)
    return u * s * jax.nn.silu(z)


def wkv7_scan(r, w, k, v, kk, a):
    B, S, H, N = r.shape

    def step(state, inp):
        r_t, w_t, k_t, v_t, kk_t, a_t = inp
        sa = jnp.einsum('bhvk,bhk->bhv', state, -kk_t)
        state = (state * w_t[:, :, None, :]
                 + sa[..., None] * (kk_t * a_t)[:, :, None, :]
                 + v_t[..., None] * k_t[:, :, None, :])
        o_t = jnp.einsum('bhvk,bhk->bhv', state, r_t)
        return state, o_t

    xs = (jnp.moveaxis(r, 1, 0), jnp.moveaxis(w, 1, 0), jnp.moveaxis(k, 1, 0),
          jnp.moveaxis(v, 1, 0), jnp.moveaxis(kk, 1, 0), jnp.moveaxis(a, 1, 0))
    state0 = jnp.zeros((B, H, N, N), jnp.float32)
    _, o = lax.scan(step, state0, xs)
    return jnp.moveaxis(o, 0, 1)


def rwkv7_branch(pb, mu, w0, decay_up, a0, iclr_up, k_k, k_a, r_k, gn_g, gn_b):
    B, S, _ = pb.shape
    f32 = jnp.float32
    p = pb.astype(f32)
    p = p + mu.astype(f32) * (token_shift(p) - p)
    W = RW_WIDTH
    r = p[..., :W]
    k = p[..., W:2 * W]
    v = p[..., 2 * W:3 * W]
    z = p[..., 3 * W:4 * W]
    wd = p[..., 4 * W:4 * W + RW_DECAY_RANK]
    ad = p[..., 4 * W + RW_DECAY_RANK:]
    decay = jnp.exp(-DECAY_SCALE * jax.nn.sigmoid(w0.astype(f32) + jnp.tanh(wd) @ decay_up.astype(f32)))
    iclr = jax.nn.sigmoid(a0.astype(f32) + ad @ iclr_up.astype(f32))

    def heads(t):
        return t.reshape(B, S, RW_HEADS, RW_HEAD_DIM)

    kk = heads(k * k_k.astype(f32))
    kk = kk * lax.rsqrt(jnp.maximum(jnp.sum(kk * kk, axis=-1, keepdims=True), L2_EPS))
    k = k * (1.0 + (iclr - 1.0) * k_a.astype(f32))
    r_h, k_h, v_h, w_h, a_h = heads(r), heads(k), heads(v), heads(decay), heads(iclr)
    o = wkv7_scan(r_h, w_h, k_h, v_h, kk, a_h)
    om = jnp.mean(o, axis=-1, keepdims=True)
    ov = jnp.mean(jnp.square(o - om), axis=-1, keepdims=True)
    o = ((o - om) * lax.rsqrt(ov + GN_EPS)).reshape(B, S, W) * gn_g.astype(f32) + gn_b.astype(f32)
    bonus = jnp.sum(r_h * k_h * r_k.astype(f32), axis=-1, keepdims=True) * v_h
    o = o + bonus.reshape(B, S, W)
    return (o * jax.nn.silu(z)).astype(pb.dtype)


def setup_inputs(seed: int = 0) -> dict:
    key = jax.random.key(seed)
    ks = jax.random.split(key, 24)
    L, D = DEPTH, D_MODEL
    nrm = jax.random.normal
    uni = jax.random.uniform
    return {
        "x": nrm(ks[0], (BATCH, SEQ, D), jnp.float32),
        "norm_pre_g": 1.0 + 0.05 * nrm(ks[1], (L, D), jnp.float32),
        "w_in": nrm(ks[2], (L, D, IN_COLS), jnp.float32) * D ** -0.5,
        "gm_ln_g": 1.0 + 0.05 * nrm(ks[3], (L, GM_WIDTH), jnp.float32),
        "gm_ln_b": 0.02 * nrm(ks[4], (L, GM_WIDTH), jnp.float32),
        "gm_w_s": nrm(ks[5], (L, GM_GROUPS, GM_BLOCK, GM_BLOCK), jnp.float32) * GM_BLOCK ** -0.5,
        "gm_b_s": 1.0 + 0.1 * nrm(ks[6], (L, GM_GROUPS, GM_BLOCK), jnp.float32),
        "rw_mu": uni(ks[7], (L, RW_COLS), jnp.float32),
        "rw_w0": uni(ks[8], (L, RW_WIDTH), jnp.float32, -3.0, 3.0),
        "rw_decay_up": 0.1 * nrm(ks[9], (L, RW_DECAY_RANK, RW_WIDTH), jnp.float32),
        "rw_a0": 0.1 * nrm(ks[10], (L, RW_WIDTH), jnp.float32),
        "rw_iclr_up": 0.1 * nrm(ks[11], (L, RW_ICLR_RANK, RW_WIDTH), jnp.float32),
        "rw_k_k": 0.85 + 0.1 * nrm(ks[12], (L, RW_WIDTH), jnp.float32),
        "rw_k_a": 1.0 + 0.1 * nrm(ks[13], (L, RW_WIDTH), jnp.float32),
        "rw_r_k": 0.1 * nrm(ks[14], (L, RW_HEADS, RW_HEAD_DIM), jnp.float32),
        "rw_gn_g": 1.0 + 0.05 * nrm(ks[15], (L, RW_WIDTH), jnp.float32),
        "rw_gn_b": 0.02 * nrm(ks[16], (L, RW_WIDTH), jnp.float32),
        "w_branch_a": nrm(ks[17], (L, GM_WIDTH, D), jnp.float32) * GM_WIDTH ** -0.5,
        "w_branch_b": nrm(ks[18], (L, RW_WIDTH, D), jnp.float32) * RW_WIDTH ** -0.5,
        "w_out": nrm(ks[19], (L, D, D), jnp.float32) * D ** -0.5,
        "norm_post_g": 1.0 + 0.05 * nrm(ks[20], (L, D), jnp.float32),
    }


def reference(x, norm_pre_g, w_in, gm_ln_g, gm_ln_b, gm_w_s, gm_b_s, rw_mu, rw_w0,
              rw_decay_up, rw_a0, rw_iclr_up, rw_k_k, rw_k_a, rw_r_k, rw_gn_g, rw_gn_b,
              w_branch_a, w_branch_b, w_out, norm_post_g):
    for l in range(DEPTH):
        h = rms_norm(x, norm_pre_g[l])
        p = h @ w_in[l]
        pa = p[..., :GM_COLS]
        pb = p[..., GM_COLS:GM_COLS + RW_COLS]
        pg = p[..., GM_COLS + RW_COLS:]
        ya = gmlp_branch(pa, gm_ln_g[l], gm_ln_b[l], gm_w_s[l], gm_b_s[l])
        yb = rwkv7_branch(pb, rw_mu[l], rw_w0[l], rw_decay_up[l], rw_a0[l], rw_iclr_up[l],
                          rw_k_k[l], rw_k_a[l], rw_r_k[l], rw_gn_g[l], rw_gn_b[l])
        gate_a = jax.nn.sigmoid(pg[..., :D_MODEL])
        gate_b = jax.nn.sigmoid(pg[..., D_MODEL:])
        merged = gate_a * (ya @ w_branch_a[l]) + gate_b * (yb @ w_branch_b[l])
        x = x + rms_norm(merged @ w_out[l], norm_post_g[l])
    return x
```

```python
import functools
import math

import jax
import jax.numpy as jnp
from jax.experimental import pallas as pl
from jax.experimental.pallas import tpu as pltpu

D_MODEL = 1024
CHUNK = 64
GM_WIDTH = 1024
GM_BLOCK = 128
GM_GROUPS = 8
RW_WIDTH = 1024
RW_HEAD_DIM = 64
RW_RANK = 64
DECAY_SCALE = math.exp(-0.5)
GM_COLS = 3 * GM_WIDTH
RW_COLS = 4 * RW_WIDTH + 2 * RW_RANK
GATE_COLS = 2 * D_MODEL
RMS_EPS = 1e-6
LN_EPS = 1e-5
GN_EPS = 64e-5
L2_EPS = 1e-12

LANES = 128
PAIRS = RW_WIDTH // LANES
RW_T = 64
SOLVE_LEVELS = 6
VMEM_LIMIT = 56 * 1024 * 1024

F32 = jnp.float32
BF16 = jnp.bfloat16


def _bdot(a, b):
    return jnp.dot(a.astype(BF16), b.astype(BF16), preferred_element_type=F32)


def _split2(x):
    hi = x.astype(BF16)
    lo = (x - hi.astype(F32)).astype(BF16)
    return hi, lo


def _split3(x):
    hi = x.astype(BF16)
    r1 = x - hi.astype(F32)
    mid = r1.astype(BF16)
    lo = (r1 - mid.astype(F32)).astype(BF16)
    return hi, mid, lo


def _inproj_kernel(x_ref, g_ref, w_ref, pa_ref, pb_ref, pg_ref):
    x = x_ref[...]
    h = x * jax.lax.rsqrt(jnp.mean(x * x, axis=-1, keepdims=True) + RMS_EPS) * g_ref[...]
    hb = h.astype(BF16)
    pa_ref[...] = jnp.dot(hb, w_ref[:, :GM_COLS], preferred_element_type=F32)
    pb_ref[...] = jnp.dot(hb, w_ref[:, GM_COLS:GM_COLS + RW_COLS], preferred_element_type=F32)
    pg_ref[...] = jnp.dot(hb, w_ref[:, GM_COLS + RW_COLS:], preferred_element_type=F32)


def _inproj(x2, g, w_bf, tm):
    n = x2.shape[0]
    in_cols = w_bf.shape[1]
    return pl.pallas_call(
        _inproj_kernel,
        grid=(n // tm,),
        in_specs=[
            pl.BlockSpec((tm, D_MODEL), lambda i: (i, 0)),
            pl.BlockSpec((1, D_MODEL), lambda i: (0, 0)),
            pl.BlockSpec((D_MODEL, in_cols), lambda i: (0, 0), pipeline_mode=pl.Buffered(1)),
        ],
        out_specs=[
            pl.BlockSpec((tm, GM_COLS), lambda i: (i, 0)),
            pl.BlockSpec((tm, RW_COLS), lambda i: (i, 0)),
            pl.BlockSpec((tm, GATE_COLS), lambda i: (i, 0)),
        ],
        out_shape=[
            jax.ShapeDtypeStruct((n, GM_COLS), F32),
            jax.ShapeDtypeStruct((n, RW_COLS), F32),
            jax.ShapeDtypeStruct((n, GATE_COLS), F32),
        ],
        compiler_params=pltpu.CompilerParams(
            dimension_semantics=("parallel",), vmem_limit_bytes=VMEM_LIMIT),
        name="inproj",
    )(x2, g, w_bf)


def _gmlp_kernel(u_ref, v_ref, z_ref, lng_ref, lnb_ref, ws_ref, bs_ref, ya_ref, *, blocks):
    u = jax.nn.gelu(u_ref[...])
    v = jax.nn.gelu(v_ref[...])
    z = z_ref[...]
    mu = jnp.mean(v, axis=-1, keepdims=True)
    vc = v - mu
    var = jnp.mean(vc * vc, axis=-1, keepdims=True)
    vn = vc * jax.lax.rsqrt(var + LN_EPS) * lng_ref[...] + lnb_ref[...]
    ri = jax.lax.broadcasted_iota(jnp.int32, (GM_BLOCK, GM_BLOCK), 0) // CHUNK
    ci = jax.lax.broadcasted_iota(jnp.int32, (GM_BLOCK, GM_BLOCK), 1) // CHUNK
    w = jnp.where((ri >= ci)[None], ws_ref[...], 0.0).astype(BF16)
    gd = GM_WIDTH // GM_GROUPS
    rows = []
    for blk in range(blocks):
        vb = vn[blk * GM_BLOCK:(blk + 1) * GM_BLOCK]
        vg = jnp.stack([vb[:, g * gd:(g + 1) * gd] for g in range(GM_GROUPS)], axis=0).astype(BF16)
        s = jnp.einsum("gij,gjc->gic", w, vg, preferred_element_type=F32) + bs_ref[...]
        rows.append(jnp.concatenate([s[g] for g in range(GM_GROUPS)], axis=1))
    s_all = jnp.concatenate(rows, axis=0)
    ya_ref[...] = u * s_all * (z * jax.nn.sigmoid(z))


def _gmlp(pa, ln_g, ln_b, w_s, b_s3, tb):
    n = pa.shape[0]
    kern = functools.partial(_gmlp_kernel, blocks=tb // GM_BLOCK)
    return pl.pallas_call(
        kern,
        grid=(n // tb,),
        in_specs=[
            pl.BlockSpec((tb, GM_WIDTH), lambda i: (i, 0)),
            pl.BlockSpec((tb, GM_WIDTH), lambda i: (i, 1)),
            pl.BlockSpec((tb, GM_WIDTH), lambda i: (i, 2)),
            pl.BlockSpec((1, GM_WIDTH), lambda i: (0, 0)),
            pl.BlockSpec((1, GM_WIDTH), lambda i: (0, 0)),
            pl.BlockSpec((GM_GROUPS, GM_BLOCK, GM_BLOCK), lambda i: (0, 0, 0)),
            pl.BlockSpec((GM_GROUPS, GM_BLOCK, 1), lambda i: (0, 0, 0)),
        ],
        out_specs=pl.BlockSpec((tb, GM_WIDTH), lambda i: (i, 0)),
        out_shape=jax.ShapeDtypeStruct((n, GM_WIDTH), F32),
        compiler_params=pltpu.CompilerParams(
            dimension_semantics=("parallel",), vmem_limit_bytes=VMEM_LIMIT),
        name="gmlp",
    )(pa, pa, pa, ln_g, ln_b, w_s, b_s3)


def _to_pairs(x):
    return jnp.stack([x[:, LANES * p:LANES * (p + 1)] for p in range(PAIRS)], axis=0)


def _from_pairs(y):
    return jnp.concatenate([y[p] for p in range(PAIRS)], axis=1)


def _rwkv_kernel(p_ref, mu_ref, w0_ref, a0_ref, lr_ref, kk_ref, ka_ref, rk_ref, gg_ref, gb_ref,
                 yb_ref, carry_ref, h_ref):
    c = pl.program_id(1)
    T = RW_T
    W = RW_WIDTH

    @pl.when(c == 0)
    def _():
        carry_ref[...] = jnp.zeros_like(carry_ref)
        h_ref[...] = jnp.zeros_like(h_ref)

    p = p_ref[0]
    row = jax.lax.broadcasted_iota(jnp.int32, p.shape, 0)
    prev = jnp.where(row == 0, carry_ref[...], pltpu.roll(p, 1, axis=0))
    carry_ref[...] = p[T - 1:T, :]
    p = p + mu_ref[...] * (prev - p)

    r = p[:, :W]
    k = p[:, W:2 * W]
    v = p[:, 2 * W:3 * W]
    z = p[:, 3 * W:4 * W]
    da = p[:, 4 * W:]
    lane = jax.lax.broadcasted_iota(jnp.int32, da.shape, 1)
    da = jnp.where(lane < RW_RANK, jnp.tanh(da), da)
    lr = _bdot(da, lr_ref[...])
    logw = -DECAY_SCALE * jax.nn.sigmoid(w0_ref[...] + lr[:, :W])
    iclr = jax.nn.sigmoid(a0_ref[...] + lr[:, W:])

    li = jax.lax.broadcasted_iota(jnp.int32, (LANES, LANES), 0) // RW_HEAD_DIM
    lj = jax.lax.broadcasted_iota(jnp.int32, (LANES, LANES), 1) // RW_HEAD_DIM
    same_head = li == lj
    seg_ones = jnp.where(same_head, 1.0, 0.0).astype(BF16)

    def seg_sum(x3):
        x2 = x3.reshape(PAIRS * T, LANES)
        hi, lo = _split2(x2)
        s = (jnp.dot(hi, seg_ones, preferred_element_type=F32)
             + jnp.dot(lo, seg_ones, preferred_element_type=F32))
        return s.reshape(PAIRS, T, LANES)

    kk = k * kk_ref[...]
    kk3 = _to_pairs(kk)
    kk3 = kk3 * jax.lax.rsqrt(jnp.maximum(seg_sum(kk3 * kk3), L2_EPS))
    iclr3 = _to_pairs(iclr)
    k3 = _to_pairs(k * (1.0 + (iclr - 1.0) * ka_ref[...]))
    r3 = _to_pairs(r)
    v3 = _to_pairs(v)

    ti = jax.lax.broadcasted_iota(jnp.int32, (T, T), 0)
    tj = jax.lax.broadcasted_iota(jnp.int32, (T, T), 1)
    tril_ones = jnp.where(ti >= tj, 1.0, 0.0).astype(BF16)
    w_hi, w_mid, w_lo = _split3(logw)
    cum = (jnp.dot(tril_ones, w_hi, preferred_element_type=F32)
           + jnp.dot(tril_ones, w_mid, preferred_element_type=F32)
           + jnp.dot(tril_ones, w_lo, preferred_element_type=F32))
    cum3 = _to_pairs(cum)
    logw3 = _to_pairs(logw)
    tot3 = cum3[:, T - 1:T, :]
    e_neg = jnp.exp(-cum3)
    e_rem = jnp.exp(tot3 - cum3)
    at = -kk3 * jnp.exp(cum3 - logw3)
    b3 = kk3 * iclr3
    bt = b3 * e_neg
    kt = k3 * e_neg
    rt = r3 * jnp.exp(cum3)
    bg = b3 * e_rem
    kg = k3 * e_rem

    lane3 = jax.lax.broadcasted_iota(jnp.int32, (1, 1, LANES), 2)
    head0 = lane3 < RW_HEAD_DIM

    def stack2(x3):
        return jnp.concatenate([jnp.where(head0, x3, 0.0), jnp.where(head0, 0.0, x3)], axis=1)

    def pairmul(l3, x3):
        return jnp.einsum("ptj,pjn->ptn", l3.astype(BF16), stack2(x3).astype(BF16),
                          preferred_element_type=F32)

    lhs = jnp.concatenate([at, rt], axis=1).astype(BF16)
    rhs = jnp.concatenate([stack2(bt), stack2(kt)], axis=1).astype(BF16)
    g = jnp.einsum("ptk,psk->pts", lhs, rhs, preferred_element_type=F32)
    trow = jax.lax.broadcasted_iota(jnp.int32, (1, T, LANES), 1)
    scol = jax.lax.broadcasted_iota(jnp.int32, (1, T, LANES), 2) % RW_HEAD_DIM
    strict = trow > scol
    incl = trow >= scol
    n_ab = jnp.where(strict, g[:, :T, :LANES], 0.0)
    a_ak = jnp.where(strict, g[:, :T, LANES:], 0.0)
    a_rb = jnp.where(incl, g[:, T:, :LANES], 0.0)
    a_rk = jnp.where(incl, g[:, T:, LANES:], 0.0)

    h0 = h_ref[...]
    ah = jnp.einsum("ptk,pkv->ptv", lhs, h0.astype(BF16), preferred_element_type=F32)
    x = ah[:, :T] + pairmul(a_ak, v3)
    pm = n_ab
    for lvl in range(SOLVE_LEVELS):
        x = x + pairmul(pm, x)
        if lvl + 1 < SOLVE_LEVELS:
            pm = pairmul(pm, pm)
    u = x
    o3 = ah[:, T:] + pairmul(a_rb, u) + pairmul(a_rk, v3)

    upd_l = jnp.swapaxes(jnp.concatenate([bg, kg], axis=1), 1, 2).astype(BF16)
    upd_r = jnp.concatenate([u, v3], axis=1).astype(BF16)
    upd = jnp.einsum("pks,psv->pkv", upd_l, upd_r, preferred_element_type=F32)
    gcol = jnp.swapaxes(jnp.broadcast_to(jnp.exp(tot3), (PAIRS, LANES, LANES)), 1, 2)
    h_ref[...] = gcol * h0 + jnp.where(same_head[None], upd, 0.0)

    inv_n = 1.0 / RW_HEAD_DIM
    om = seg_sum(o3) * inv_n
    oc = o3 - om
    ov = seg_sum(oc * oc) * inv_n
    on = _from_pairs(oc * jax.lax.rsqrt(ov + GN_EPS)) * gg_ref[...] + gb_ref[...]
    bonus = seg_sum(r3 * k3 * _to_pairs(jnp.broadcast_to(rk_ref[...], (T, W)))) * v3
    yb_ref[0] = (on + _from_pairs(bonus)) * (z * jax.nn.sigmoid(z))


def _rwkv(pb3, mu, w0, a0, lr_w, k_k, k_a, r_k, gn_g, gn_b):
    bsz, seq, _ = pb3.shape
    row = lambda n: pl.BlockSpec((1, n), lambda b, c: (0, 0))
    return pl.pallas_call(
        _rwkv_kernel,
        grid=(bsz, seq // RW_T),
        in_specs=[
            pl.BlockSpec((1, RW_T, RW_COLS), lambda b, c: (b, c, 0)),
            row(RW_COLS), row(RW_WIDTH), row(RW_WIDTH),
            pl.BlockSpec((2 * RW_RANK, 2 * RW_WIDTH), lambda b, c: (0, 0)),
            row(RW_WIDTH), row(RW_WIDTH), row(RW_WIDTH), row(RW_WIDTH), row(RW_WIDTH),
        ],
        out_specs=pl.BlockSpec((1, RW_T, RW_WIDTH), lambda b, c: (b, c, 0)),
        out_shape=jax.ShapeDtypeStruct((bsz, seq, RW_WIDTH), F32),
        scratch_shapes=[
            pltpu.VMEM((1, RW_COLS), F32),
            pltpu.VMEM((PAIRS, LANES, LANES), F32),
        ],
        compiler_params=pltpu.CompilerParams(
            dimension_semantics=("parallel", "arbitrary"), vmem_limit_bytes=VMEM_LIMIT),
        name="rwkv7",
    )(pb3, mu, w0, a0, lr_w, k_k, k_a, r_k, gn_g, gn_b)


def _out_kernel(x_ref, ya_ref, yb_ref, ga_ref, gb_ref, wa_ref, wb_ref, wo_ref, g_ref, o_ref):
    ma = _bdot(ya_ref[...], wa_ref[...])
    mb = _bdot(yb_ref[...], wb_ref[...])
    merged = jax.nn.sigmoid(ga_ref[...]) * ma + jax.nn.sigmoid(gb_ref[...]) * mb
    y = _bdot(merged, wo_ref[...])
    yn = y * jax.lax.rsqrt(jnp.mean(y * y, axis=-1, keepdims=True) + RMS_EPS) * g_ref[...]
    o_ref[...] = x_ref[...] + yn


def _out(x2, ya, yb, pg, wa, wb, wo, g, tm):
    n = x2.shape[0]
    tok = lambda j: pl.BlockSpec((tm, D_MODEL), lambda i: (i, j))
    wspec = pl.BlockSpec((D_MODEL, D_MODEL), lambda i: (0, 0))
    return pl.pallas_call(
        _out_kernel,
        grid=(n // tm,),
        in_specs=[tok(0), tok(0), tok(0), tok(0), tok(1), wspec, wspec, wspec,
                  pl.BlockSpec((1, D_MODEL), lambda i: (0, 0))],
        out_specs=tok(0),
        out_shape=jax.ShapeDtypeStruct((n, D_MODEL), F32),
        compiler_params=pltpu.CompilerParams(
            dimension_semantics=("parallel",), vmem_limit_bytes=VMEM_LIMIT),
        name="merge_out",
    )(x2, ya, yb, pg, pg, wa, wb, wo, g)


def _layer(x, norm_pre_g, w_in, gm_ln_g, gm_ln_b, gm_w_s, gm_b_s, rw_mu, rw_w0, rw_decay_up, rw_a0,
           rw_iclr_up, rw_k_k, rw_k_a, rw_r_k, rw_gn_g, rw_gn_b, w_branch_a, w_branch_b, w_out,
           norm_post_g):
    bsz, seq, d = x.shape
    n = bsz * seq
    x2 = x.reshape(n, d)
    row = lambda a: a.reshape(1, -1)

    pa, pb, pg = _inproj(x2, row(norm_pre_g), w_in.astype(BF16), tm=min(256, n))
    ya = _gmlp(pa, row(gm_ln_g), row(gm_ln_b), gm_w_s, gm_b_s[:, :, None], tb=min(256, n))

    zeros = jnp.zeros_like(rw_decay_up)
    lr_w = jnp.concatenate([jnp.concatenate([rw_decay_up, zeros], axis=1),
                            jnp.concatenate([zeros, rw_iclr_up], axis=1)], axis=0).astype(BF16)
    yb = _rwkv(pb.reshape(bsz, seq, RW_COLS), row(rw_mu), row(rw_w0), row(rw_a0), lr_w,
               row(rw_k_k), row(rw_k_a), row(rw_r_k), row(rw_gn_g), row(rw_gn_b))

    out = _out(x2, ya, yb.reshape(n, RW_WIDTH), pg, w_branch_a.astype(BF16), w_branch_b.astype(BF16),
               w_out.astype(BF16), row(norm_post_g), tm=min(512, n))
    return out.reshape(bsz, seq, d)


def kernel(x, norm_pre_g, w_in, gm_ln_g, gm_ln_b, gm_w_s, gm_b_s, rw_mu, rw_w0, rw_decay_up, rw_a0,
           rw_iclr_up, rw_k_k, rw_k_a, rw_r_k, rw_gn_g, rw_gn_b, w_branch_a, w_branch_b, w_out,
           norm_post_g):
    depth = norm_pre_g.shape[0]
    for l in range(depth):
        x = _layer(x, norm_pre_g[l], w_in[l], gm_ln_g[l], gm_ln_b[l], gm_w_s[l], gm_b_s[l], rw_mu[l],
                   rw_w0[l], rw_decay_up[l], rw_a0[l], rw_iclr_up[l], rw_k_k[l], rw_k_a[l],
                   rw_r_k[l], rw_gn_g[l], rw_gn_b[l], w_branch_a[l], w_branch_b[l], w_out[l],
                   norm_post_g[l])
    return x
```

```python
import functools
import math

import jax
import jax.numpy as jnp
from jax.experimental import pallas as pl
from jax.experimental.pallas import tpu as pltpu

D_MODEL = 1024
CHUNK = 64
GM_WIDTH = 1024
GM_BLOCK = 128
GM_GROUPS = 8
RW_WIDTH = 1024
RW_HEAD_DIM = 64
RW_RANK = 64
DECAY_SCALE = math.exp(-0.5)
GM_COLS = 3 * GM_WIDTH
RW_COLS = 4 * RW_WIDTH + 2 * RW_RANK
GATE_COLS = 2 * D_MODEL
RMS_EPS = 1e-6
LN_EPS = 1e-5
GN_EPS = 64e-5
L2_EPS = 1e-12

LANES = 128
PAIRS = RW_WIDTH // LANES
RW_T = 64
RW_NB = 4
SOLVE_LEVELS = 6
VMEM_LIMIT = 56 * 1024 * 1024

F32 = jnp.float32
BF16 = jnp.bfloat16


def _bdot(a, b):
    return jnp.dot(a.astype(BF16), b.astype(BF16), preferred_element_type=F32)


def _split2(x):
    hi = x.astype(BF16)
    lo = (x - hi.astype(F32)).astype(BF16)
    return hi, lo


def _split3(x):
    hi = x.astype(BF16)
    r1 = x - hi.astype(F32)
    mid = r1.astype(BF16)
    lo = (r1 - mid.astype(F32)).astype(BF16)
    return hi, mid, lo


def _inproj_kernel(x_ref, g_ref, w_ref, pa_ref, pb_ref, pg_ref):
    x = x_ref[...]
    h = x * jax.lax.rsqrt(jnp.mean(x * x, axis=-1, keepdims=True) + RMS_EPS) * g_ref[...]
    hb = h.astype(BF16)
    pa_ref[...] = jnp.dot(hb, w_ref[:, :GM_COLS], preferred_element_type=F32)
    pb_ref[...] = jnp.dot(hb, w_ref[:, GM_COLS:GM_COLS + RW_COLS], preferred_element_type=F32)
    pg_ref[...] = jnp.dot(hb, w_ref[:, GM_COLS + RW_COLS:], preferred_element_type=F32)


def _inproj(x2, g, w_bf, tm):
    n = x2.shape[0]
    in_cols = w_bf.shape[1]
    return pl.pallas_call(
        _inproj_kernel,
        grid=(n // tm,),
        in_specs=[
            pl.BlockSpec((tm, D_MODEL), lambda i: (i, 0)),
            pl.BlockSpec((1, D_MODEL), lambda i: (0, 0)),
            pl.BlockSpec((D_MODEL, in_cols), lambda i: (0, 0), pipeline_mode=pl.Buffered(1)),
        ],
        out_specs=[
            pl.BlockSpec((tm, GM_COLS), lambda i: (i, 0)),
            pl.BlockSpec((tm, RW_COLS), lambda i: (i, 0)),
            pl.BlockSpec((tm, GATE_COLS), lambda i: (i, 0)),
        ],
        out_shape=[
            jax.ShapeDtypeStruct((n, GM_COLS), F32),
            jax.ShapeDtypeStruct((n, RW_COLS), F32),
            jax.ShapeDtypeStruct((n, GATE_COLS), F32),
        ],
        compiler_params=pltpu.CompilerParams(
            dimension_semantics=("parallel",), vmem_limit_bytes=VMEM_LIMIT),
        name="inproj",
    )(x2, g, w_bf)


def _gmlp_kernel(u_ref, v_ref, z_ref, lng_ref, lnb_ref, ws_ref, bs_ref, ya_ref, *, blocks):
    u = jax.nn.gelu(u_ref[...])
    v = jax.nn.gelu(v_ref[...])
    z = z_ref[...]
    mu = jnp.mean(v, axis=-1, keepdims=True)
    vc = v - mu
    var = jnp.mean(vc * vc, axis=-1, keepdims=True)
    vn = vc * jax.lax.rsqrt(var + LN_EPS) * lng_ref[...] + lnb_ref[...]
    ri = jax.lax.broadcasted_iota(jnp.int32, (GM_BLOCK, GM_BLOCK), 0) // CHUNK
    ci = jax.lax.broadcasted_iota(jnp.int32, (GM_BLOCK, GM_BLOCK), 1) // CHUNK
    w = jnp.where((ri >= ci)[None], ws_ref[...], 0.0).astype(BF16)
    gd = GM_WIDTH // GM_GROUPS
    rows = []
    for blk in range(blocks):
        vb = vn[blk * GM_BLOCK:(blk + 1) * GM_BLOCK]
        vg = jnp.stack([vb[:, g * gd:(g + 1) * gd] for g in range(GM_GROUPS)], axis=0).astype(BF16)
        s = jnp.einsum("gij,gjc->gic", w, vg, preferred_element_type=F32) + bs_ref[...]
        rows.append(jnp.concatenate([s[g] for g in range(GM_GROUPS)], axis=1))
    s_all = jnp.concatenate(rows, axis=0)
    ya_ref[...] = u * s_all * (z * jax.nn.sigmoid(z))


def _gmlp(pa, ln_g, ln_b, w_s, b_s3, tb):
    n = pa.shape[0]
    kern = functools.partial(_gmlp_kernel, blocks=tb // GM_BLOCK)
    return pl.pallas_call(
        kern,
        grid=(n // tb,),
        in_specs=[
            pl.BlockSpec((tb, GM_WIDTH), lambda i: (i, 0)),
            pl.BlockSpec((tb, GM_WIDTH), lambda i: (i, 1)),
            pl.BlockSpec((tb, GM_WIDTH), lambda i: (i, 2)),
            pl.BlockSpec((1, GM_WIDTH), lambda i: (0, 0)),
            pl.BlockSpec((1, GM_WIDTH), lambda i: (0, 0)),
            pl.BlockSpec((GM_GROUPS, GM_BLOCK, GM_BLOCK), lambda i: (0, 0, 0)),
            pl.BlockSpec((GM_GROUPS, GM_BLOCK, 1), lambda i: (0, 0, 0)),
        ],
        out_specs=pl.BlockSpec((tb, GM_WIDTH), lambda i: (i, 0)),
        out_shape=jax.ShapeDtypeStruct((n, GM_WIDTH), F32),
        compiler_params=pltpu.CompilerParams(
            dimension_semantics=("parallel",), vmem_limit_bytes=VMEM_LIMIT),
        name="gmlp",
    )(pa, pa, pa, ln_g, ln_b, w_s, b_s3)


def _to_pairs(x, nb):
    t = x.shape[0] // nb
    return jnp.stack([x[n * t:(n + 1) * t, LANES * p:LANES * (p + 1)]
                      for n in range(nb) for p in range(PAIRS)], axis=0)


def _from_pairs(y, nb):
    return jnp.concatenate(
        [jnp.concatenate([y[n * PAIRS + p] for p in range(PAIRS)], axis=1) for n in range(nb)], axis=0)


def _rwkv_kernel(p_ref, mu_ref, w0_ref, a0_ref, lr_ref, kk_ref, ka_ref, rk_ref, gg_ref, gb_ref,
                 yb_ref, carry_ref, h_ref, *, nb):
    T = RW_T
    W = RW_WIDTH
    NP = nb * PAIRS

    @pl.when(pl.program_id(1) == 0)
    def _():
        carry_ref[...] = jnp.zeros_like(carry_ref)
        h_ref[...] = jnp.zeros_like(h_ref)

    p3 = p_ref[...]
    row = jax.lax.broadcasted_iota(jnp.int32, p3.shape, 1)
    prev = jnp.where(row == 0, carry_ref[...], pltpu.roll(p3, 1, axis=1))
    carry_ref[...] = p3[:, T - 1:T, :]
    p = (p3 + mu_ref[...] * (prev - p3)).reshape(nb * T, RW_COLS)

    r = p[:, :W]
    k = p[:, W:2 * W]
    v = p[:, 2 * W:3 * W]
    z = p[:, 3 * W:4 * W]
    da = p[:, 4 * W:]
    lane = jax.lax.broadcasted_iota(jnp.int32, da.shape, 1)
    da = jnp.where(lane < RW_RANK, jnp.tanh(da), da)
    lr = _bdot(da, lr_ref[...])
    logw = -DECAY_SCALE * jax.nn.sigmoid(w0_ref[...] + lr[:, :W])
    iclr = jax.nn.sigmoid(a0_ref[...] + lr[:, W:])

    li = jax.lax.broadcasted_iota(jnp.int32, (LANES, LANES), 0) // RW_HEAD_DIM
    lj = jax.lax.broadcasted_iota(jnp.int32, (LANES, LANES), 1) // RW_HEAD_DIM
    same_head = li == lj

    lane3 = jax.lax.broadcasted_iota(jnp.int32, (1, 1, LANES), 2)
    head0 = lane3 < RW_HEAD_DIM

    def seg_sum(x3):
        s0 = jnp.sum(jnp.where(head0, x3, 0.0), axis=-1, keepdims=True)
        s1 = jnp.sum(jnp.where(head0, 0.0, x3), axis=-1, keepdims=True)
        return jnp.where(head0, s0, s1)

    kk = k * kk_ref[...]
    kk3 = _to_pairs(kk, nb)
    kk3 = kk3 * jax.lax.rsqrt(jnp.maximum(seg_sum(kk3 * kk3), L2_EPS))
    iclr3 = _to_pairs(iclr, nb)
    k3 = _to_pairs(k * (1.0 + (iclr - 1.0) * ka_ref[...]), nb)
    r3 = _to_pairs(r, nb)
    v3 = _to_pairs(v, nb)

    ti = jax.lax.broadcasted_iota(jnp.int32, (nb * T, nb * T), 0)
    tj = jax.lax.broadcasted_iota(jnp.int32, (nb * T, nb * T), 1)
    tril_ones = jnp.where((ti >= tj) & (ti // T == tj // T), 1.0, 0.0).astype(BF16)
    w_hi, w_mid, w_lo = _split3(logw)
    cum = (jnp.dot(tril_ones, w_hi, preferred_element_type=F32)
           + jnp.dot(tril_ones, w_mid, preferred_element_type=F32)
           + jnp.dot(tril_ones, w_lo, preferred_element_type=F32))
    cum3 = _to_pairs(cum, nb)
    logw3 = _to_pairs(logw, nb)
    tot3 = cum3[:, T - 1:T, :]
    e_neg = jnp.exp(-cum3)
    e_rem = jnp.exp(tot3 - cum3)
    at = -kk3 * jnp.exp(cum3 - logw3)
    b3 = kk3 * iclr3
    bt = b3 * e_neg
    kt = k3 * e_neg
    rt = r3 * jnp.exp(cum3)
    bg = b3 * e_rem
    kg = k3 * e_rem

    def stack2(x3):
        return jnp.concatenate([jnp.where(head0, x3, 0.0), jnp.where(head0, 0.0, x3)], axis=1)

    def pairmul(l3, x3):
        return jnp.einsum("ptj,pjn->ptn", l3.astype(BF16), stack2(x3).astype(BF16),
                          preferred_element_type=F32)

    lhs = jnp.concatenate([at, rt], axis=1).astype(BF16)
    rhs = jnp.concatenate([stack2(bt), stack2(kt)], axis=1).astype(BF16)
    g = jnp.einsum("ptk,psk->pts", lhs, rhs, preferred_element_type=F32)
    trow = jax.lax.broadcasted_iota(jnp.int32, (1, T, LANES), 1)
    scol = jax.lax.broadcasted_iota(jnp.int32, (1, T, LANES), 2) % RW_HEAD_DIM
    strict = trow > scol
    incl = trow >= scol
    n_ab = jnp.where(strict, g[:, :T, :LANES], 0.0)
    a_ak = jnp.where(strict, g[:, :T, LANES:], 0.0)
    a_rb = jnp.where(incl, g[:, T:, :LANES], 0.0)
    a_rk = jnp.where(incl, g[:, T:, LANES:], 0.0)

    h0 = h_ref[...]
    ah = jnp.einsum("ptk,pkv->ptv", lhs, h0.astype(BF16), preferred_element_type=F32)
    x = ah[:, :T] + pairmul(a_ak, v3)
    pm = n_ab
    for lvl in range(SOLVE_LEVELS):
        x = x + pairmul(pm, x)
        if lvl + 1 < SOLVE_LEVELS:
            pm = pairmul(pm, pm)
    u = x
    o3 = ah[:, T:] + pairmul(a_rb, u) + pairmul(a_rk, v3)

    upd_l = jnp.swapaxes(jnp.concatenate([bg, kg], axis=1), 1, 2).astype(BF16)
    upd_r = jnp.concatenate([u, v3], axis=1).astype(BF16)
    upd = jnp.einsum("pks,psv->pkv", upd_l, upd_r, preferred_element_type=F32)
    gcol = jnp.swapaxes(jnp.broadcast_to(jnp.exp(tot3), (NP, LANES, LANES)), 1, 2)
    h_ref[...] = gcol * h0 + jnp.where(same_head[None], upd, 0.0)

    inv_n = 1.0 / RW_HEAD_DIM
    om = seg_sum(o3) * inv_n
    oc = o3 - om
    ov = seg_sum(oc * oc) * inv_n
    on = _from_pairs(oc * jax.lax.rsqrt(ov + GN_EPS), nb) * gg_ref[...] + gb_ref[...]
    bonus = seg_sum(r3 * k3 * _to_pairs(jnp.broadcast_to(rk_ref[...], (nb * T, W)), nb)) * v3
    yb = (on + _from_pairs(bonus, nb)) * (z * jax.nn.sigmoid(z))
    yb_ref[...] = yb.reshape(nb, T, W)


def _rwkv(pb3, mu, w0, a0, lr_w, k_k, k_a, r_k, gn_g, gn_b, nb):
    bsz, seq, _ = pb3.shape
    row = lambda n: pl.BlockSpec((1, n), lambda b, c: (0, 0))
    return pl.pallas_call(
        functools.partial(_rwkv_kernel, nb=nb),
        grid=(bsz // nb, seq // RW_T),
        in_specs=[
            pl.BlockSpec((nb, RW_T, RW_COLS), lambda b, c: (b, c, 0)),
            row(RW_COLS), row(RW_WIDTH), row(RW_WIDTH),
            pl.BlockSpec((2 * RW_RANK, 2 * RW_WIDTH), lambda b, c: (0, 0)),
            row(RW_WIDTH), row(RW_WIDTH), row(RW_WIDTH), row(RW_WIDTH), row(RW_WIDTH),
        ],
        out_specs=pl.BlockSpec((nb, RW_T, RW_WIDTH), lambda b, c: (b, c, 0)),
        out_shape=jax.ShapeDtypeStruct((bsz, seq, RW_WIDTH), F32),
        scratch_shapes=[
            pltpu.VMEM((nb, 1, RW_COLS), F32),
            pltpu.VMEM((nb * PAIRS, LANES, LANES), F32),
        ],
        compiler_params=pltpu.CompilerParams(
            dimension_semantics=("parallel", "arbitrary"), vmem_limit_bytes=VMEM_LIMIT),
        name="rwkv7",
    )(pb3, mu, w0, a0, lr_w, k_k, k_a, r_k, gn_g, gn_b)


def _out_kernel(x_ref, ya_ref, yb_ref, ga_ref, gb_ref, wa_ref, wb_ref, wo_ref, g_ref, o_ref):
    ma = _bdot(ya_ref[...], wa_ref[...])
    mb = _bdot(yb_ref[...], wb_ref[...])
    merged = jax.nn.sigmoid(ga_ref[...]) * ma + jax.nn.sigmoid(gb_ref[...]) * mb
    y = _bdot(merged, wo_ref[...])
    yn = y * jax.lax.rsqrt(jnp.mean(y * y, axis=-1, keepdims=True) + RMS_EPS) * g_ref[...]
    o_ref[...] = x_ref[...] + yn


def _out(x2, ya, yb, pg, wa, wb, wo, g, tm):
    n = x2.shape[0]
    tok = lambda j: pl.BlockSpec((tm, D_MODEL), lambda i: (i, j))
    wspec = pl.BlockSpec((D_MODEL, D_MODEL), lambda i: (0, 0))
    return pl.pallas_call(
        _out_kernel,
        grid=(n // tm,),
        in_specs=[tok(0), tok(0), tok(0), tok(0), tok(1), wspec, wspec, wspec,
                  pl.BlockSpec((1, D_MODEL), lambda i: (0, 0))],
        out_specs=tok(0),
        out_shape=jax.ShapeDtypeStruct((n, D_MODEL), F32),
        compiler_params=pltpu.CompilerParams(
            dimension_semantics=("parallel",), vmem_limit_bytes=VMEM_LIMIT),
        name="merge_out",
    )(x2, ya, yb, pg, pg, wa, wb, wo, g)


def _layer(x, norm_pre_g, w_in, gm_ln_g, gm_ln_b, gm_w_s, gm_b_s, rw_mu, rw_w0, rw_decay_up, rw_a0,
           rw_iclr_up, rw_k_k, rw_k_a, rw_r_k, rw_gn_g, rw_gn_b, w_branch_a, w_branch_b, w_out,
           norm_post_g):
    bsz, seq, d = x.shape
    n = bsz * seq
    x2 = x.reshape(n, d)
    row = lambda a: a.reshape(1, -1)

    pa, pb, pg = _inproj(x2, row(norm_pre_g), w_in.astype(BF16), tm=min(256, n))
    ya = _gmlp(pa, row(gm_ln_g), row(gm_ln_b), gm_w_s, gm_b_s[:, :, None], tb=min(256, n))

    zeros = jnp.zeros_like(rw_decay_up)
    lr_w = jnp.concatenate([jnp.concatenate([rw_decay_up, zeros], axis=1),
                            jnp.concatenate([zeros, rw_iclr_up], axis=1)], axis=0).astype(BF16)
    yb = _rwkv(pb.reshape(bsz, seq, RW_COLS), row(rw_mu), row(rw_w0), row(rw_a0), lr_w,
               row(rw_k_k), row(rw_k_a), row(rw_r_k), row(rw_gn_g), row(rw_gn_b),
               nb=RW_NB if bsz % RW_NB == 0 else 1)

    out = _out(x2, ya, yb.reshape(n, RW_WIDTH), pg, w_branch_a.astype(BF16), w_branch_b.astype(BF16),
               w_out.astype(BF16), row(norm_post_g), tm=min(512, n))
    return out.reshape(bsz, seq, d)


def kernel(x, norm_pre_g, w_in, gm_ln_g, gm_ln_b, gm_w_s, gm_b_s, rw_mu, rw_w0, rw_decay_up, rw_a0,
           rw_iclr_up, rw_k_k, rw_k_a, rw_r_k, rw_gn_g, rw_gn_b, w_branch_a, w_branch_b, w_out,
           norm_post_g):
    depth = norm_pre_g.shape[0]
    for l in range(depth):
        x = _layer(x, norm_pre_g[l], w_in[l], gm_ln_g[l], gm_ln_b[l], gm_w_s[l], gm_b_s[l], rw_mu[l],
                   rw_w0[l], rw_decay_up[l], rw_a0[l], rw_iclr_up[l], rw_k_k[l], rw_k_a[l],
                   rw_r_k[l], rw_gn_g[l], rw_gn_b[l], w_branch_a[l], w_branch_b[l], w_out[l],
                   norm_post_g[l])
    return x
```

```python
import functools
import math

import jax
import jax.numpy as jnp
from jax.experimental import pallas as pl
from jax.experimental.pallas import tpu as pltpu

D_MODEL = 1024
CHUNK = 64
GM_WIDTH = 1024
GM_BLOCK = 128
GM_GROUPS = 8
RW_WIDTH = 1024
RW_HEAD_DIM = 64
RW_RANK = 64
DECAY_SCALE = math.exp(-0.5)
GM_COLS = 3 * GM_WIDTH
RW_COLS = 4 * RW_WIDTH + 2 * RW_RANK
GATE_COLS = 2 * D_MODEL
RMS_EPS = 1e-6
LN_EPS = 1e-5
GN_EPS = 64e-5
L2_EPS = 1e-12

LANES = 128
PAIRS = RW_WIDTH // LANES
RW_T = 64
RW_NB = 4
SOLVE_LEVELS = 6
VMEM_LIMIT = 56 * 1024 * 1024

F32 = jnp.float32
BF16 = jnp.bfloat16


def _bdot(a, b):
    return jnp.dot(a.astype(BF16), b.astype(BF16), preferred_element_type=F32)


def _split3(x):
    hi = x.astype(BF16)
    r1 = x - hi.astype(F32)
    mid = r1.astype(BF16)
    lo = (r1 - mid.astype(F32)).astype(BF16)
    return hi, mid, lo


def _front_kernel(x_ref, g_ref, w_ref, lng_ref, lnb_ref, ws_ref, bs_ref, ya_ref, pb_ref, gate_ref, *,
                  blocks):
    x = x_ref[...]
    h = x * jax.lax.rsqrt(jnp.mean(x * x, axis=-1, keepdims=True) + RMS_EPS) * g_ref[...]
    hb = h.astype(BF16)

    pa = jnp.dot(hb, w_ref[:, :GM_COLS], preferred_element_type=F32)
    u = jax.nn.gelu(pa[:, :GM_WIDTH])
    v = jax.nn.gelu(pa[:, GM_WIDTH:2 * GM_WIDTH])
    z = pa[:, 2 * GM_WIDTH:]
    mu = jnp.mean(v, axis=-1, keepdims=True)
    vc = v - mu
    var = jnp.mean(vc * vc, axis=-1, keepdims=True)
    vn = vc * jax.lax.rsqrt(var + LN_EPS) * lng_ref[...] + lnb_ref[...]
    ri = jax.lax.broadcasted_iota(jnp.int32, (GM_BLOCK, GM_BLOCK), 0) // CHUNK
    ci = jax.lax.broadcasted_iota(jnp.int32, (GM_BLOCK, GM_BLOCK), 1) // CHUNK
    w = jnp.where((ri >= ci)[None], ws_ref[...], 0.0).astype(BF16)
    gd = GM_WIDTH // GM_GROUPS
    rows = []
    for blk in range(blocks):
        vb = vn[blk * GM_BLOCK:(blk + 1) * GM_BLOCK]
        vg = jnp.stack([vb[:, g * gd:(g + 1) * gd] for g in range(GM_GROUPS)], axis=0).astype(BF16)
        s = jnp.einsum("gij,gjc->gic", w, vg, preferred_element_type=F32) + bs_ref[...]
        rows.append(jnp.concatenate([s[g] for g in range(GM_GROUPS)], axis=1))
    s_all = jnp.concatenate(rows, axis=0)
    ya_ref[...] = (u * s_all * (z * jax.nn.sigmoid(z))).astype(ya_ref.dtype)

    pb_ref[...] = jnp.dot(hb, w_ref[:, GM_COLS:GM_COLS + RW_COLS],
                          preferred_element_type=F32).astype(pb_ref.dtype)
    pg = jnp.dot(hb, w_ref[:, GM_COLS + RW_COLS:], preferred_element_type=F32)
    gate_ref[...] = jax.nn.sigmoid(pg).astype(gate_ref.dtype)


def _front(x2, g, w_bf, ln_g, ln_b, w_s, b_s3, tm):
    n = x2.shape[0]
    in_cols = w_bf.shape[1]
    const2 = lambda i: (0, 0)
    const3 = lambda i: (0, 0, 0)
    return pl.pallas_call(
        functools.partial(_front_kernel, blocks=tm // GM_BLOCK),
        grid=(n // tm,),
        in_specs=[
            pl.BlockSpec((tm, D_MODEL), lambda i: (i, 0)),
            pl.BlockSpec((1, D_MODEL), const2),
            pl.BlockSpec((D_MODEL, in_cols), const2, pipeline_mode=pl.Buffered(1)),
            pl.BlockSpec((1, GM_WIDTH), const2),
            pl.BlockSpec((1, GM_WIDTH), const2),
            pl.BlockSpec((GM_GROUPS, GM_BLOCK, GM_BLOCK), const3),
            pl.BlockSpec((GM_GROUPS, GM_BLOCK, 1), const3),
        ],
        out_specs=[
            pl.BlockSpec((tm, GM_WIDTH), lambda i: (i, 0)),
            pl.BlockSpec((tm, RW_COLS), lambda i: (i, 0)),
            pl.BlockSpec((tm, GATE_COLS), lambda i: (i, 0)),
        ],
        out_shape=[
            jax.ShapeDtypeStruct((n, GM_WIDTH), BF16),
            jax.ShapeDtypeStruct((n, RW_COLS), F32),
            jax.ShapeDtypeStruct((n, GATE_COLS), BF16),
        ],
        compiler_params=pltpu.CompilerParams(
            dimension_semantics=("parallel",), vmem_limit_bytes=VMEM_LIMIT),
        name="front",
    )(x2, g, w_bf, ln_g, ln_b, w_s, b_s3)


def _to_pairs(x, nb):
    t = x.shape[0] // nb
    return jnp.stack([x[n * t:(n + 1) * t, LANES * p:LANES * (p + 1)]
                      for n in range(nb) for p in range(PAIRS)], axis=0)


def _from_pairs(y, nb):
    return jnp.concatenate(
        [jnp.concatenate([y[n * PAIRS + p] for p in range(PAIRS)], axis=1) for n in range(nb)], axis=0)


def _rwkv_kernel(p_ref, mu_ref, w0_ref, a0_ref, lr_ref, kk_ref, ka_ref, rk_ref, gg_ref, gb_ref,
                 yb_ref, carry_ref, h_ref, *, nb):
    T = RW_T
    W = RW_WIDTH
    NP = nb * PAIRS

    @pl.when(pl.program_id(1) == 0)
    def _():
        carry_ref[...] = jnp.zeros_like(carry_ref)
        h_ref[...] = jnp.zeros_like(h_ref)

    p3 = p_ref[...]
    row = jax.lax.broadcasted_iota(jnp.int32, p3.shape, 1)
    prev = jnp.where(row == 0, carry_ref[...], pltpu.roll(p3, 1, axis=1))
    carry_ref[...] = p3[:, T - 1:T, :]
    p = (p3 + mu_ref[...] * (prev - p3)).reshape(nb * T, RW_COLS)

    r = p[:, :W]
    k = p[:, W:2 * W]
    v = p[:, 2 * W:3 * W]
    z = p[:, 3 * W:4 * W]
    da = p[:, 4 * W:]
    lane = jax.lax.broadcasted_iota(jnp.int32, da.shape, 1)
    da = jnp.where(lane < RW_RANK, jnp.tanh(da), da)
    lr = _bdot(da, lr_ref[...])
    logw = -DECAY_SCALE * jax.nn.sigmoid(w0_ref[...] + lr[:, :W])
    iclr = jax.nn.sigmoid(a0_ref[...] + lr[:, W:])

    li = jax.lax.broadcasted_iota(jnp.int32, (LANES, LANES), 0) // RW_HEAD_DIM
    lj = jax.lax.broadcasted_iota(jnp.int32, (LANES, LANES), 1) // RW_HEAD_DIM
    same_head = li == lj
    lane3 = jax.lax.broadcasted_iota(jnp.int32, (1, 1, LANES), 2)
    head0 = lane3 < RW_HEAD_DIM

    def seg_sum(x3):
        s0 = jnp.sum(jnp.where(head0, x3, 0.0), axis=-1, keepdims=True)
        s1 = jnp.sum(jnp.where(head0, 0.0, x3), axis=-1, keepdims=True)
        return jnp.where(head0, s0, s1)

    kk3 = _to_pairs(k * kk_ref[...], nb)
    kk3 = kk3 * jax.lax.rsqrt(jnp.maximum(seg_sum(kk3 * kk3), L2_EPS))
    iclr3 = _to_pairs(iclr, nb)
    k3 = _to_pairs(k * (1.0 + (iclr - 1.0) * ka_ref[...]), nb)
    r3 = _to_pairs(r, nb)
    v3 = _to_pairs(v, nb)

    ti = jax.lax.broadcasted_iota(jnp.int32, (nb * T, nb * T), 0)
    tj = jax.lax.broadcasted_iota(jnp.int32, (nb * T, nb * T), 1)
    tril_ones = jnp.where((ti >= tj) & (ti // T == tj // T), 1.0, 0.0).astype(BF16)
    w_hi, w_mid, w_lo = _split3(logw)
    cum = (jnp.dot(tril_ones, w_hi, preferred_element_type=F32)
           + jnp.dot(tril_ones, w_mid, preferred_element_type=F32)
           + jnp.dot(tril_ones, w_lo, preferred_element_type=F32))
    cum3 = _to_pairs(cum, nb)
    logw3 = _to_pairs(logw, nb)
    tot3 = cum3[:, T - 1:T, :]
    e_neg = jnp.exp(-cum3)
    e_rem = jnp.exp(tot3 - cum3)
    at = -kk3 * jnp.exp(cum3 - logw3)
    b3 = kk3 * iclr3
    bt = b3 * e_neg
    kt = k3 * e_neg
    rt = r3 * jnp.exp(cum3)
    bg = b3 * e_rem
    kg = k3 * e_rem

    def stack2(x3):
        return jnp.concatenate([jnp.where(head0, x3, 0.0), jnp.where(head0, 0.0, x3)], axis=1)

    def pairmul(l3, x3):
        return jnp.einsum("ptj,pjn->ptn", l3.astype(BF16), stack2(x3).astype(BF16),
                          preferred_element_type=F32)

    lhs = jnp.concatenate([at, rt], axis=1).astype(BF16)
    rhs = jnp.concatenate([stack2(bt), stack2(kt)], axis=1).astype(BF16)
    g = jnp.einsum("ptk,psk->pts", lhs, rhs, preferred_element_type=F32)
    trow = jax.lax.broadcasted_iota(jnp.int32, (1, T, LANES), 1)
    scol = jax.lax.broadcasted_iota(jnp.int32, (1, T, LANES), 2) % RW_HEAD_DIM
    strict = trow > scol
    incl = trow >= scol
    n_ab = jnp.where(strict, g[:, :T, :LANES], 0.0)
    a_ak = jnp.where(strict, g[:, :T, LANES:], 0.0)
    a_rb = jnp.where(incl, g[:, T:, :LANES], 0.0)
    a_rk = jnp.where(incl, g[:, T:, LANES:], 0.0)

    h0 = h_ref[...]
    ah = jnp.einsum("ptk,pkv->ptv", lhs, h0.astype(BF16), preferred_element_type=F32)
    x = ah[:, :T] + pairmul(a_ak, v3)
    pm = n_ab
    for lvl in range(SOLVE_LEVELS):
        x = x + pairmul(pm, x)
        if lvl + 1 < SOLVE_LEVELS:
            pm = pairmul(pm, pm)
    u = x
    o3 = ah[:, T:] + pairmul(a_rb, u) + pairmul(a_rk, v3)

    upd_l = jnp.swapaxes(jnp.concatenate([bg, kg], axis=1), 1, 2).astype(BF16)
    upd_r = jnp.concatenate([u, v3], axis=1).astype(BF16)
    upd = jnp.einsum("pks,psv->pkv", upd_l, upd_r, preferred_element_type=F32)
    gcol = jnp.swapaxes(jnp.broadcast_to(jnp.exp(tot3), (NP, LANES, LANES)), 1, 2)
    h_ref[...] = gcol * h0 + jnp.where(same_head[None], upd, 0.0)

    inv_n = 1.0 / RW_HEAD_DIM
    om = seg_sum(o3) * inv_n
    oc = o3 - om
    ov = seg_sum(oc * oc) * inv_n
    on = _from_pairs(oc * jax.lax.rsqrt(ov + GN_EPS), nb) * gg_ref[...] + gb_ref[...]
    bonus = seg_sum(r3 * k3 * _to_pairs(jnp.broadcast_to(rk_ref[...], (nb * T, W)), nb)) * v3
    yb = (on + _from_pairs(bonus, nb)) * (z * jax.nn.sigmoid(z))
    yb_ref[...] = yb.reshape(nb, T, W).astype(yb_ref.dtype)


def _rwkv(pb3, mu, w0, a0, lr_w, k_k, k_a, r_k, gn_g, gn_b, nb):
    bsz, seq, _ = pb3.shape
    row = lambda n: pl.BlockSpec((1, n), lambda b, c: (0, 0))
    return pl.pallas_call(
        functools.partial(_rwkv_kernel, nb=nb),
        grid=(bsz // nb, seq // RW_T),
        in_specs=[
            pl.BlockSpec((nb, RW_T, RW_COLS), lambda b, c: (b, c, 0)),
            row(RW_COLS), row(RW_WIDTH), row(RW_WIDTH),
            pl.BlockSpec((2 * RW_RANK, 2 * RW_WIDTH), lambda b, c: (0, 0)),
            row(RW_WIDTH), row(RW_WIDTH), row(RW_WIDTH), row(RW_WIDTH), row(RW_WIDTH),
        ],
        out_specs=pl.BlockSpec((nb, RW_T, RW_WIDTH), lambda b, c: (b, c, 0)),
        out_shape=jax.ShapeDtypeStruct((bsz, seq, RW_WIDTH), BF16),
        scratch_shapes=[
            pltpu.VMEM((nb, 1, RW_COLS), F32),
            pltpu.VMEM((nb * PAIRS, LANES, LANES), F32),
        ],
        compiler_params=pltpu.CompilerParams(
            dimension_semantics=("parallel", "arbitrary"), vmem_limit_bytes=VMEM_LIMIT),
        name="rwkv7",
    )(pb3, mu, w0, a0, lr_w, k_k, k_a, r_k, gn_g, gn_b)


def _out_kernel(x_ref, ya_ref, yb_ref, ga_ref, gb_ref, wa_ref, wb_ref, wo_ref, g_ref, o_ref):
    ma = _bdot(ya_ref[...], wa_ref[...])
    mb = _bdot(yb_ref[...], wb_ref[...])
    merged = ga_ref[...].astype(F32) * ma + gb_ref[...].astype(F32) * mb
    y = _bdot(merged, wo_ref[...])
    yn = y * jax.lax.rsqrt(jnp.mean(y * y, axis=-1, keepdims=True) + RMS_EPS) * g_ref[...]
    o_ref[...] = x_ref[...] + yn


def _out(x2, ya, yb, gates, wa, wb, wo, g, tm):
    n = x2.shape[0]
    tok = lambda j: pl.BlockSpec((tm, D_MODEL), lambda i: (i, j))
    wspec = pl.BlockSpec((D_MODEL, D_MODEL), lambda i: (0, 0))
    return pl.pallas_call(
        _out_kernel,
        grid=(n // tm,),
        in_specs=[tok(0), tok(0), tok(0), tok(0), tok(1), wspec, wspec, wspec,
                  pl.BlockSpec((1, D_MODEL), lambda i: (0, 0))],
        out_specs=tok(0),
        out_shape=jax.ShapeDtypeStruct((n, D_MODEL), F32),
        compiler_params=pltpu.CompilerParams(
            dimension_semantics=("parallel",), vmem_limit_bytes=VMEM_LIMIT),
        name="merge_out",
    )(x2, ya, yb, gates, gates, wa, wb, wo, g)


def _layer(x, norm_pre_g, w_in, gm_ln_g, gm_ln_b, gm_w_s, gm_b_s, rw_mu, rw_w0, rw_decay_up, rw_a0,
           rw_iclr_up, rw_k_k, rw_k_a, rw_r_k, rw_gn_g, rw_gn_b, w_branch_a, w_branch_b, w_out,
           norm_post_g):
    bsz, seq, d = x.shape
    n = bsz * seq
    x2 = x.reshape(n, d)
    row = lambda a: a.reshape(1, -1)

    ya, pb, gates = _front(x2, row(norm_pre_g), w_in.astype(BF16), row(gm_ln_g), row(gm_ln_b), gm_w_s,
                           gm_b_s[:, :, None], tm=min(256, n))

    zeros = jnp.zeros_like(rw_decay_up)
    lr_w = jnp.concatenate([jnp.concatenate([rw_decay_up, zeros], axis=1),
                            jnp.concatenate([zeros, rw_iclr_up], axis=1)], axis=0).astype(BF16)
    yb = _rwkv(pb.reshape(bsz, seq, RW_COLS), row(rw_mu), row(rw_w0), row(rw_a0), lr_w,
               row(rw_k_k), row(rw_k_a), row(rw_r_k), row(rw_gn_g), row(rw_gn_b),
               nb=RW_NB if bsz % RW_NB == 0 else 1)

    out = _out(x2, ya, yb.reshape(n, RW_WIDTH), gates, w_branch_a.astype(BF16), w_branch_b.astype(BF16),
               w_out.astype(BF16), row(norm_post_g), tm=min(512, n))
    return out.reshape(bsz, seq, d)


def kernel(x, norm_pre_g, w_in, gm_ln_g, gm_ln_b, gm_w_s, gm_b_s, rw_mu, rw_w0, rw_decay_up, rw_a0,
           rw_iclr_up, rw_k_k, rw_k_a, rw_r_k, rw_gn_g, rw_gn_b, w_branch_a, w_branch_b, w_out,
           norm_post_g):
    depth = norm_pre_g.shape[0]
    for l in range(depth):
        x = _layer(x, norm_pre_g[l], w_in[l], gm_ln_g[l], gm_ln_b[l], gm_w_s[l], gm_b_s[l], rw_mu[l],
                   rw_w0[l], rw_decay_up[l], rw_a0[l], rw_iclr_up[l], rw_k_k[l], rw_k_a[l],
                   rw_r_k[l], rw_gn_g[l], rw_gn_b[l], w_branch_a[l], w_branch_b[l], w_out[l],
                   norm_post_g[l])
    return x
```

```python
import functools
import math

import jax
import jax.numpy as jnp
from jax.experimental import pallas as pl
from jax.experimental.pallas import tpu as pltpu

D_MODEL = 1024
CHUNK = 64
GM_WIDTH = 1024
GM_BLOCK = 128
GM_GROUPS = 8
RW_WIDTH = 1024
RW_HEAD_DIM = 64
RW_RANK = 64
DECAY_SCALE = math.exp(-0.5)
GM_COLS = 3 * GM_WIDTH
RW_COLS = 4 * RW_WIDTH + 2 * RW_RANK
GATE_COLS = 2 * D_MODEL
RMS_EPS = 1e-6
LN_EPS = 1e-5
GN_EPS = 64e-5
L2_EPS = 1e-12

LANES = 128
PAIRS = RW_WIDTH // LANES
RW_T = 64
FRONT_TM = 256
RW_NB = 4
SOLVE_LEVELS = 6
VMEM_LIMIT = 56 * 1024 * 1024

F32 = jnp.float32
BF16 = jnp.bfloat16


def _bdot(a, b):
    return jnp.dot(a.astype(BF16), b.astype(BF16), preferred_element_type=F32)


def _split3(x):
    hi = x.astype(BF16)
    r1 = x - hi.astype(F32)
    mid = r1.astype(BF16)
    lo = (r1 - mid.astype(F32)).astype(BF16)
    return hi, mid, lo


def _to_pairs(x, nb):
    t = x.shape[0] // nb
    return jnp.stack([x[n * t:(n + 1) * t, LANES * p:LANES * (p + 1)]
                      for n in range(nb) for p in range(PAIRS)], axis=0)


def _from_pairs(y, nb):
    return jnp.concatenate(
        [jnp.concatenate([y[n * PAIRS + p] for p in range(PAIRS)], axis=1) for n in range(nb)], axis=0)


def _head0_mask():
    return jax.lax.broadcasted_iota(jnp.int32, (1, 1, LANES), 2) < RW_HEAD_DIM


def _seg_sum(x3):
    head0 = _head0_mask()
    s0 = jnp.sum(jnp.where(head0, x3, 0.0), axis=-1, keepdims=True)
    s1 = jnp.sum(jnp.where(head0, 0.0, x3), axis=-1, keepdims=True)
    return jnp.where(head0, s0, s1)


def _front_kernel(x_ref, g_ref, w_ref, lng_ref, lnb_ref, ws_ref, bs_ref, mu_ref, w0_ref, a0_ref, lr_ref,
                  kk_ref, ka_ref, rk_ref,
                  ya_ref, gate_ref, at_ref, rt_ref, bt_ref, kt_ref, v_ref, bg_ref, kg_ref, bonus_ref,
                  zg_ref, etot_ref, carry_ref, *, tm, tiles_per_seq):
    T = RW_T
    W = RW_WIDTH
    x = x_ref[...]
    h = x * jax.lax.rsqrt(jnp.mean(x * x, axis=-1, keepdims=True) + RMS_EPS) * g_ref[...]
    hb = h.astype(BF16)

    pb = jnp.dot(hb, w_ref[:, GM_COLS:GM_COLS + RW_COLS], preferred_element_type=F32)
    first = jax.lax.rem(pl.program_id(0), tiles_per_seq) == 0
    row = jax.lax.broadcasted_iota(jnp.int32, pb.shape, 0)
    carry = jnp.where(first, 0.0, carry_ref[...])
    prev = jnp.where(row == 0, carry, pltpu.roll(pb, 1, axis=0))
    carry_ref[...] = pb[tm - 1:tm, :]
    p = pb + mu_ref[...] * (prev - pb)

    r = p[:, :W]
    k = p[:, W:2 * W]
    v = p[:, 2 * W:3 * W]
    z = p[:, 3 * W:4 * W]
    da = p[:, 4 * W:]
    lane = jax.lax.broadcasted_iota(jnp.int32, da.shape, 1)
    da = jnp.where(lane < RW_RANK, jnp.tanh(da), da)
    lr = _bdot(da, lr_ref[...])
    logw = -DECAY_SCALE * jax.nn.sigmoid(w0_ref[...] + lr[:, :W])
    iclr = jax.nn.sigmoid(a0_ref[...] + lr[:, W:])

    def seg_sum_rows(y):
        return _from_pairs(_seg_sum(_to_pairs(y, 1)), 1)

    kk = k * kk_ref[...]
    kk = kk * jax.lax.rsqrt(jnp.maximum(seg_sum_rows(kk * kk), L2_EPS))
    km = k * (1.0 + (iclr - 1.0) * ka_ref[...])

    ti = jax.lax.broadcasted_iota(jnp.int32, (tm, tm), 0)
    tj = jax.lax.broadcasted_iota(jnp.int32, (tm, tm), 1)
    tril_ones = jnp.where((ti >= tj) & (ti // T == tj // T), 1.0, 0.0).astype(BF16)
    w_hi, w_mid, w_lo = _split3(logw)
    cum = (jnp.dot(tril_ones, w_hi, preferred_element_type=F32)
           + jnp.dot(tril_ones, w_mid, preferred_element_type=F32)
           + jnp.dot(tril_ones, w_lo, preferred_element_type=F32))
    nck = tm // T
    tot = cum.reshape(nck, T, W)[:, T - 1:T, :]
    rem = (tot - cum.reshape(nck, T, W)).reshape(tm, W)
    e_neg = jnp.exp(-cum)
    e_rem = jnp.exp(rem)
    b = kk * iclr
    at_ref[...] = (-kk * jnp.exp(cum - logw)).astype(BF16)
    rt_ref[...] = (r * jnp.exp(cum)).astype(BF16)
    bt_ref[...] = (b * e_neg).astype(BF16)
    kt_ref[...] = (km * e_neg).astype(BF16)
    v_ref[...] = v.astype(BF16)
    bg_ref[...] = (b * e_rem).astype(BF16)
    kg_ref[...] = (km * e_rem).astype(BF16)
    etot_ref[...] = jnp.exp(tot)
    bonus_ref[...] = (seg_sum_rows(r * km * rk_ref[...]) * v).astype(BF16)
    zg_ref[...] = (z * jax.nn.sigmoid(z)).astype(BF16)

    pa = jnp.dot(hb, w_ref[:, :GM_COLS], preferred_element_type=F32)
    u = jax.nn.gelu(pa[:, :GM_WIDTH])
    gv = jax.nn.gelu(pa[:, GM_WIDTH:2 * GM_WIDTH])
    gz = pa[:, 2 * GM_WIDTH:]
    mean = jnp.mean(gv, axis=-1, keepdims=True)
    vc = gv - mean
    var = jnp.mean(vc * vc, axis=-1, keepdims=True)
    vn = vc * jax.lax.rsqrt(var + LN_EPS) * lng_ref[...] + lnb_ref[...]
    ri = jax.lax.broadcasted_iota(jnp.int32, (GM_BLOCK, GM_BLOCK), 0) // CHUNK
    ci = jax.lax.broadcasted_iota(jnp.int32, (GM_BLOCK, GM_BLOCK), 1) // CHUNK
    w = jnp.where((ri >= ci)[None], ws_ref[...], 0.0).astype(BF16)
    gd = GM_WIDTH // GM_GROUPS
    rows = []
    for blk in range(tm // GM_BLOCK):
        vb = vn[blk * GM_BLOCK:(blk + 1) * GM_BLOCK]
        vg = jnp.stack([vb[:, g * gd:(g + 1) * gd] for g in range(GM_GROUPS)], axis=0).astype(BF16)
        s = jnp.einsum("gij,gjc->gic", w, vg, preferred_element_type=F32) + bs_ref[...]
        rows.append(jnp.concatenate([s[g] for g in range(GM_GROUPS)], axis=1))
    s_all = jnp.concatenate(rows, axis=0)
    ya_ref[...] = (u * s_all * (gz * jax.nn.sigmoid(gz))).astype(ya_ref.dtype)

    pg = jnp.dot(hb, w_ref[:, GM_COLS + RW_COLS:], preferred_element_type=F32)
    gate_ref[...] = jax.nn.sigmoid(pg).astype(gate_ref.dtype)


def _front(x2, seq, g, w_bf, ln_g, ln_b, w_s, b_s3, mu, w0, a0, lr_w, k_k, k_a, r_k, tm):
    n = x2.shape[0]
    in_cols = w_bf.shape[1]
    const2 = lambda i: (0, 0)
    const3 = lambda i: (0, 0, 0)
    row = lambda c: pl.BlockSpec((1, c), const2)
    tok = lambda c: pl.BlockSpec((tm, c), lambda i: (i, 0))
    nck = tm // RW_T
    bf = lambda c: jax.ShapeDtypeStruct((n, c), BF16)
    return pl.pallas_call(
        functools.partial(_front_kernel, tm=tm, tiles_per_seq=seq // tm),
        grid=(n // tm,),
        in_specs=[
            tok(D_MODEL), row(D_MODEL),
            pl.BlockSpec((D_MODEL, in_cols), const2, pipeline_mode=pl.Buffered(1)),
            row(GM_WIDTH), row(GM_WIDTH),
            pl.BlockSpec((GM_GROUPS, GM_BLOCK, GM_BLOCK), const3),
            pl.BlockSpec((GM_GROUPS, GM_BLOCK, 1), const3),
            row(RW_COLS), row(RW_WIDTH), row(RW_WIDTH),
            pl.BlockSpec((2 * RW_RANK, 2 * RW_WIDTH), const2),
            row(RW_WIDTH), row(RW_WIDTH), row(RW_WIDTH),
        ],
        out_specs=[tok(GM_WIDTH), tok(GATE_COLS)] + [tok(RW_WIDTH)] * 9
                  + [pl.BlockSpec((nck, 1, RW_WIDTH), lambda i: (i, 0, 0))],
        out_shape=[bf(GM_WIDTH), bf(GATE_COLS)] + [bf(RW_WIDTH)] * 9
                  + [jax.ShapeDtypeStruct((n // RW_T, 1, RW_WIDTH), F32)],
        scratch_shapes=[pltpu.VMEM((1, RW_COLS), F32)],
        compiler_params=pltpu.CompilerParams(
            dimension_semantics=("arbitrary",), vmem_limit_bytes=VMEM_LIMIT),
        name="front",
    )(x2, g, w_bf, ln_g, ln_b, w_s, b_s3, mu, w0, a0, lr_w, k_k, k_a, r_k)


def _rwkv_kernel(at_ref, rt_ref, bt_ref, kt_ref, v_ref, bg_ref, kg_ref, bonus_ref, zg_ref, etot_ref,
                 gg_ref, gb_ref, yb_ref, h_ref, *, nb):
    T = RW_T
    W = RW_WIDTH
    NP = nb * PAIRS

    @pl.when(pl.program_id(1) == 0)
    def _():
        h_ref[...] = jnp.zeros_like(h_ref)

    def pairs(ref):
        return _to_pairs(ref[...].reshape(nb * T, W), nb)

    head0 = _head0_mask()

    def stack2(x3):
        zero = jnp.zeros_like(x3)
        return jnp.concatenate([jnp.where(head0, x3, zero), jnp.where(head0, zero, x3)], axis=1)

    def pairmul(l3, x3):
        return jnp.einsum("ptj,pjn->ptn", l3.astype(BF16), stack2(x3.astype(BF16)),
                          preferred_element_type=F32)

    v3 = pairs(v_ref)

    lhs = jnp.concatenate([pairs(at_ref), pairs(rt_ref)], axis=1)
    rhs = jnp.concatenate([stack2(pairs(bt_ref)), stack2(pairs(kt_ref))], axis=1)
    g = jnp.einsum("ptk,psk->pts", lhs, rhs, preferred_element_type=F32)
    trow = jax.lax.broadcasted_iota(jnp.int32, (1, T, LANES), 1)
    scol = jax.lax.broadcasted_iota(jnp.int32, (1, T, LANES), 2) % RW_HEAD_DIM
    strict = trow > scol
    incl = trow >= scol
    n_ab = jnp.where(strict, g[:, :T, :LANES], 0.0)
    a_ak = jnp.where(strict, g[:, :T, LANES:], 0.0)
    a_rb = jnp.where(incl, g[:, T:, :LANES], 0.0)
    a_rk = jnp.where(incl, g[:, T:, LANES:], 0.0)

    h0 = h_ref[...]
    ah = jnp.einsum("ptk,pkv->ptv", lhs, h0.astype(BF16), preferred_element_type=F32)
    x = ah[:, :T] + pairmul(a_ak, v3)
    pm = n_ab
    for lvl in range(SOLVE_LEVELS):
        x = x + pairmul(pm, x)
        if lvl + 1 < SOLVE_LEVELS:
            pm = pairmul(pm, pm)
    u = x
    o3 = ah[:, T:] + pairmul(a_rb, u) + pairmul(a_rk, v3)

    upd_l = jnp.swapaxes(jnp.concatenate([pairs(bg_ref), pairs(kg_ref)], axis=1).astype(F32), 1, 2)
    upd_r = jnp.concatenate([u.astype(BF16), v3], axis=1)
    upd = jnp.einsum("pks,psv->pkv", upd_l.astype(BF16), upd_r, preferred_element_type=F32)
    etot3 = _to_pairs(etot_ref[...].reshape(nb, W), nb)
    gcol = jnp.swapaxes(jnp.broadcast_to(etot3, (NP, LANES, LANES)), 1, 2)
    li = jax.lax.broadcasted_iota(jnp.int32, (1, LANES, LANES), 1) // RW_HEAD_DIM
    lj = jax.lax.broadcasted_iota(jnp.int32, (1, LANES, LANES), 2) // RW_HEAD_DIM
    h_ref[...] = gcol * h0 + jnp.where(li == lj, upd, 0.0)

    inv_n = 1.0 / RW_HEAD_DIM
    oc = o3 - _seg_sum(o3) * inv_n
    ov = _seg_sum(oc * oc) * inv_n
    on = _from_pairs(oc * jax.lax.rsqrt(ov + GN_EPS), nb) * gg_ref[...] + gb_ref[...]
    bonus = bonus_ref[...].reshape(nb * T, W).astype(F32)
    zg = zg_ref[...].reshape(nb * T, W).astype(F32)
    yb_ref[...] = ((on + bonus) * zg).reshape(nb, T, W).astype(yb_ref.dtype)


def _rwkv(staged, etot, gn_g, gn_b, nb):
    bsz, seq, _ = staged[0].shape
    row = pl.BlockSpec((1, RW_WIDTH), lambda b, c: (0, 0))
    blk = pl.BlockSpec((nb, RW_T, RW_WIDTH), lambda b, c: (b, c, 0))
    return pl.pallas_call(
        functools.partial(_rwkv_kernel, nb=nb),
        grid=(bsz // nb, seq // RW_T),
        in_specs=[blk] * 9 + [pl.BlockSpec((nb, 1, 1, RW_WIDTH), lambda b, c: (b, c, 0, 0)), row, row],
        out_specs=blk,
        out_shape=jax.ShapeDtypeStruct((bsz, seq, RW_WIDTH), BF16),
        scratch_shapes=[
            pltpu.VMEM((nb * PAIRS, LANES, LANES), F32),
        ],
        compiler_params=pltpu.CompilerParams(
            dimension_semantics=("parallel", "arbitrary"), vmem_limit_bytes=VMEM_LIMIT),
        name="rwkv7",
    )(*staged, etot, gn_g, gn_b)


def _out_kernel(x_ref, ya_ref, yb_ref, ga_ref, gb_ref, wa_ref, wb_ref, wo_ref, g_ref, o_ref):
    ma = _bdot(ya_ref[...], wa_ref[...])
    mb = _bdot(yb_ref[...], wb_ref[...])
    merged = ga_ref[...].astype(F32) * ma + gb_ref[...].astype(F32) * mb
    y = _bdot(merged, wo_ref[...])
    yn = y * jax.lax.rsqrt(jnp.mean(y * y, axis=-1, keepdims=True) + RMS_EPS) * g_ref[...]
    o_ref[...] = x_ref[...] + yn


def _out(x2, ya, yb, gates, wa, wb, wo, g, tm):
    n = x2.shape[0]
    tok = lambda j: pl.BlockSpec((tm, D_MODEL), lambda i: (i, j))
    wspec = pl.BlockSpec((D_MODEL, D_MODEL), lambda i: (0, 0))
    return pl.pallas_call(
        _out_kernel,
        grid=(n // tm,),
        in_specs=[tok(0), tok(0), tok(0), tok(0), tok(1), wspec, wspec, wspec,
                  pl.BlockSpec((1, D_MODEL), lambda i: (0, 0))],
        out_specs=tok(0),
        out_shape=jax.ShapeDtypeStruct((n, D_MODEL), F32),
        compiler_params=pltpu.CompilerParams(
            dimension_semantics=("parallel",), vmem_limit_bytes=VMEM_LIMIT),
        name="merge_out",
    )(x2, ya, yb, gates, gates, wa, wb, wo, g)


def _layer(x, norm_pre_g, w_in, gm_ln_g, gm_ln_b, gm_w_s, gm_b_s, rw_mu, rw_w0, rw_decay_up, rw_a0,
           rw_iclr_up, rw_k_k, rw_k_a, rw_r_k, rw_gn_g, rw_gn_b, w_branch_a, w_branch_b, w_out,
           norm_post_g):
    bsz, seq, d = x.shape
    n = bsz * seq
    x2 = x.reshape(n, d)
    row = lambda a: a.reshape(1, -1)

    zeros = jnp.zeros_like(rw_decay_up)
    lr_w = jnp.concatenate([jnp.concatenate([rw_decay_up, zeros], axis=1),
                            jnp.concatenate([zeros, rw_iclr_up], axis=1)], axis=0).astype(BF16)
    ya, gates, *staged, etot = _front(
        x2, seq, row(norm_pre_g), w_in.astype(BF16), row(gm_ln_g), row(gm_ln_b), gm_w_s,
        gm_b_s[:, :, None], row(rw_mu), row(rw_w0), row(rw_a0), lr_w, row(rw_k_k), row(rw_k_a),
        row(rw_r_k), tm=min(FRONT_TM, seq))
    yb = _rwkv([a.reshape(bsz, seq, RW_WIDTH) for a in staged],
               etot.reshape(bsz, seq // RW_T, 1, RW_WIDTH), row(rw_gn_g), row(rw_gn_b),
               nb=RW_NB if bsz % RW_NB == 0 else 1)

    out = _out(x2, ya, yb.reshape(n, RW_WIDTH), gates, w_branch_a.astype(BF16), w_branch_b.astype(BF16),
               w_out.astype(BF16), row(norm_post_g), tm=min(512, n))
    return out.reshape(bsz, seq, d)


def kernel(x, norm_pre_g, w_in, gm_ln_g, gm_ln_b, gm_w_s, gm_b_s, rw_mu, rw_w0, rw_decay_up, rw_a0,
           rw_iclr_up, rw_k_k, rw_k_a, rw_r_k, rw_gn_g, rw_gn_b, w_branch_a, w_branch_b, w_out,
           norm_post_g):
    depth = norm_pre_g.shape[0]
    for l in range(depth):
        x = _layer(x, norm_pre_g[l], w_in[l], gm_ln_g[l], gm_ln_b[l], gm_w_s[l], gm_b_s[l], rw_mu[l],
                   rw_w0[l], rw_decay_up[l], rw_a0[l], rw_iclr_up[l], rw_k_k[l], rw_k_a[l],
                   rw_r_k[l], rw_gn_g[l], rw_gn_b[l], w_branch_a[l], w_branch_b[l], w_out[l],
                   norm_post_g[l])
    return x
```

```python
import functools
import math

import jax
import jax.numpy as jnp
from jax.experimental import pallas as pl
from jax.experimental.pallas import tpu as pltpu

D_MODEL = 1024
CHUNK = 64
GM_WIDTH = 1024
GM_BLOCK = 128
GM_GROUPS = 8
RW_WIDTH = 1024
RW_HEAD_DIM = 64
RW_RANK = 64
DECAY_SCALE = math.exp(-0.5)
GM_COLS = 3 * GM_WIDTH
RW_COLS = 4 * RW_WIDTH + 2 * RW_RANK
GATE_COLS = 2 * D_MODEL
RMS_EPS = 1e-6
LN_EPS = 1e-5
GN_EPS = 64e-5
L2_EPS = 1e-12

LANES = 128
PAIRS = RW_WIDTH // LANES
RW_T = 64
PAIR_COLS = 4 * LANES
FRONT_TM = 256
RW_NB = 4
SOLVE_LEVELS = 6
VMEM_LIMIT = 56 * 1024 * 1024

F32 = jnp.float32
BF16 = jnp.bfloat16


def _bdot(a, b):
    return jnp.dot(a.astype(BF16), b.astype(BF16), preferred_element_type=F32)


def _to_pairs(x, nb):
    t = x.shape[0] // nb
    return jnp.stack([x[n * t:(n + 1) * t, LANES * p:LANES * (p + 1)]
                      for n in range(nb) for p in range(PAIRS)], axis=0)


def _from_pairs(y, nb):
    return jnp.concatenate(
        [jnp.concatenate([y[n * PAIRS + p] for p in range(PAIRS)], axis=1) for n in range(nb)], axis=0)


def _head0_mask():
    return jax.lax.broadcasted_iota(jnp.int32, (1, 1, LANES), 2) < RW_HEAD_DIM


def _seg_sum(x3):
    head0 = _head0_mask()
    s0 = jnp.sum(jnp.where(head0, x3, 0.0), axis=-1, keepdims=True)
    s1 = jnp.sum(jnp.where(head0, 0.0, x3), axis=-1, keepdims=True)
    return jnp.where(head0, s0, s1)


def _front_kernel(x_ref, g_ref, w_ref, lng_ref, lnb_ref, ws_ref, bs_ref, mu_ref, w0_ref, a0_ref, lr_ref,
                  kk_ref, ka_ref, rk_ref,
                  ya_ref, gate_ref, at_ref, rt_ref, bt_ref, kt_ref, v_ref, bonus_ref,
                  zg_ref, etot_ref, carry_ref, *, tm, tiles_per_seq):
    T = RW_T
    W = RW_WIDTH
    x = x_ref[...]
    h = x * jax.lax.rsqrt(jnp.mean(x * x, axis=-1, keepdims=True) + RMS_EPS) * g_ref[...]
    hb = h.astype(BF16)

    first = jax.lax.rem(pl.program_id(0), tiles_per_seq) == 0

    def project_shifted(lo, width):
        cols = slice(lo, lo + width)
        pc = jnp.dot(hb, w_ref[:, cols], preferred_element_type=F32)
        row = jax.lax.broadcasted_iota(jnp.int32, pc.shape, 0)
        carry = jnp.where(first, 0.0, carry_ref[:, cols])
        prev = jnp.where(row == 0, carry, pltpu.roll(pc, 1, axis=0))
        carry_ref[:, cols] = pc[tm - 1:tm, :]
        return pc + mu_ref[:, cols] * (prev - pc)

    da = project_shifted(PAIRS * PAIR_COLS, 2 * RW_RANK)
    lane = jax.lax.broadcasted_iota(jnp.int32, da.shape, 1)
    da = jnp.where(lane < RW_RANK, jnp.tanh(da), da)
    lr = _bdot(da, lr_ref[...])

    ti = jax.lax.broadcasted_iota(jnp.int32, (T, T), 0)
    tj = jax.lax.broadcasted_iota(jnp.int32, (T, T), 1)
    tril_ones = jnp.where(ti >= tj, 1.0, 0.0).astype(BF16)
    head0 = jax.lax.broadcasted_iota(jnp.int32, (1, LANES), 1) < RW_HEAD_DIM

    def seg_sum(y):
        s0 = jnp.sum(jnp.where(head0, y, 0.0), axis=-1, keepdims=True)
        s1 = jnp.sum(jnp.where(head0, 0.0, y), axis=-1, keepdims=True)
        return jnp.where(head0, s0, s1)

    for q in range(PAIRS):
        cols = slice(q * LANES, (q + 1) * LANES)
        pq = project_shifted(q * PAIR_COLS, PAIR_COLS)
        for c in range(tm // T):
            rows = slice(c * T, (c + 1) * T)
            r = pq[rows, :LANES]
            k = pq[rows, LANES:2 * LANES]
            v = pq[rows, 2 * LANES:3 * LANES]
            z = pq[rows, 3 * LANES:]
            logw = -DECAY_SCALE * jax.nn.sigmoid(w0_ref[:, cols] + lr[rows, cols])
            iclr = jax.nn.sigmoid(a0_ref[:, cols] + lr[rows, W + q * LANES:W + (q + 1) * LANES])

            kk = k * kk_ref[:, cols]
            kk = kk * jax.lax.rsqrt(jnp.maximum(seg_sum(kk * kk), L2_EPS))
            km = k * (1.0 + (iclr - 1.0) * ka_ref[:, cols])

            w_hi = logw.astype(BF16)
            w_lo = (logw - w_hi.astype(F32)).astype(BF16)
            cum = (jnp.dot(tril_ones, w_hi, preferred_element_type=F32)
                   + jnp.dot(tril_ones, w_lo, preferred_element_type=F32))
            e_neg = jnp.exp(-cum)
            at_ref[rows, cols] = (-kk * jnp.exp(cum - logw)).astype(BF16)
            rt_ref[rows, cols] = (r * jnp.exp(cum)).astype(BF16)
            bt_ref[rows, cols] = (kk * iclr * e_neg).astype(BF16)
            kt_ref[rows, cols] = (km * e_neg).astype(BF16)
            v_ref[rows, cols] = v.astype(BF16)
            etot_ref[c, :, cols] = jnp.exp(cum[T - 1:T, :])
            bonus_ref[rows, cols] = (seg_sum(r * km * rk_ref[:, cols]) * v).astype(BF16)
            zg_ref[rows, cols] = (z * jax.nn.sigmoid(z)).astype(BF16)

    pa = jnp.dot(hb, w_ref[:, RW_COLS:RW_COLS + GM_COLS], preferred_element_type=F32)
    u = jax.nn.gelu(pa[:, :GM_WIDTH])
    gv = jax.nn.gelu(pa[:, GM_WIDTH:2 * GM_WIDTH])
    gz = pa[:, 2 * GM_WIDTH:]
    mean = jnp.mean(gv, axis=-1, keepdims=True)
    vc = gv - mean
    var = jnp.mean(vc * vc, axis=-1, keepdims=True)
    vn = vc * jax.lax.rsqrt(var + LN_EPS) * lng_ref[...] + lnb_ref[...]
    ri = jax.lax.broadcasted_iota(jnp.int32, (GM_BLOCK, GM_BLOCK), 0) // CHUNK
    ci = jax.lax.broadcasted_iota(jnp.int32, (GM_BLOCK, GM_BLOCK), 1) // CHUNK
    w = jnp.where((ri >= ci)[None], ws_ref[...], 0.0).astype(BF16)
    gd = GM_WIDTH // GM_GROUPS
    rows = []
    for blk in range(tm // GM_BLOCK):
        vb = vn[blk * GM_BLOCK:(blk + 1) * GM_BLOCK]
        vg = jnp.stack([vb[:, g * gd:(g + 1) * gd] for g in range(GM_GROUPS)], axis=0).astype(BF16)
        s = jnp.einsum("gij,gjc->gic", w, vg, preferred_element_type=F32) + bs_ref[...]
        rows.append(jnp.concatenate([s[g] for g in range(GM_GROUPS)], axis=1))
    s_all = jnp.concatenate(rows, axis=0)
    ya_ref[...] = (u * s_all * (gz * jax.nn.sigmoid(gz))).astype(ya_ref.dtype)

    pg = jnp.dot(hb, w_ref[:, RW_COLS + GM_COLS:], preferred_element_type=F32)
    gate_ref[...] = jax.nn.sigmoid(pg).astype(gate_ref.dtype)


def _front(x2, seq, g, w_bf, ln_g, ln_b, w_s, b_s3, mu, w0, a0, lr_w, k_k, k_a, r_k, tm):
    n = x2.shape[0]
    in_cols = w_bf.shape[1]
    const2 = lambda i: (0, 0)
    const3 = lambda i: (0, 0, 0)
    row = lambda c: pl.BlockSpec((1, c), const2)
    tok = lambda c: pl.BlockSpec((tm, c), lambda i: (i, 0))
    nck = tm // RW_T
    bf = lambda c: jax.ShapeDtypeStruct((n, c), BF16)
    return pl.pallas_call(
        functools.partial(_front_kernel, tm=tm, tiles_per_seq=seq // tm),
        grid=(n // tm,),
        in_specs=[
            tok(D_MODEL), row(D_MODEL),
            pl.BlockSpec((D_MODEL, in_cols), const2, pipeline_mode=pl.Buffered(1)),
            row(GM_WIDTH), row(GM_WIDTH),
            pl.BlockSpec((GM_GROUPS, GM_BLOCK, GM_BLOCK), const3),
            pl.BlockSpec((GM_GROUPS, GM_BLOCK, 1), const3),
            row(RW_COLS), row(RW_WIDTH), row(RW_WIDTH),
            pl.BlockSpec((2 * RW_RANK, 2 * RW_WIDTH), const2),
            row(RW_WIDTH), row(RW_WIDTH), row(RW_WIDTH),
        ],
        out_specs=[tok(GM_WIDTH), tok(GATE_COLS)] + [tok(RW_WIDTH)] * 7
                  + [pl.BlockSpec((nck, 1, RW_WIDTH), lambda i: (i, 0, 0))],
        out_shape=[bf(GM_WIDTH), bf(GATE_COLS)] + [bf(RW_WIDTH)] * 7
                  + [jax.ShapeDtypeStruct((n // RW_T, 1, RW_WIDTH), F32)],
        scratch_shapes=[pltpu.VMEM((1, RW_COLS), F32)],
        compiler_params=pltpu.CompilerParams(
            dimension_semantics=("arbitrary",), vmem_limit_bytes=VMEM_LIMIT),
        name="front",
    )(x2, g, w_bf, ln_g, ln_b, w_s, b_s3, mu, w0, a0, lr_w, k_k, k_a, r_k)


def _rwkv_kernel(at_ref, rt_ref, bt_ref, kt_ref, v_ref, bonus_ref, zg_ref, etot_ref,
                 gg_ref, gb_ref, yb_ref, h_ref, *, nb):
    T = RW_T
    W = RW_WIDTH
    NP = nb * PAIRS

    @pl.when(pl.program_id(1) == 0)
    def _():
        h_ref[...] = jnp.zeros_like(h_ref)

    def pairs(ref):
        return _to_pairs(ref[...].reshape(nb * T, W), nb)

    head0 = _head0_mask()

    def stack2(x3):
        zero = jnp.zeros_like(x3)
        return jnp.concatenate([jnp.where(head0, x3, zero), jnp.where(head0, zero, x3)], axis=1)

    def pairmul(l3, x3):
        return jnp.einsum("ptj,pjn->ptn", l3.astype(BF16), stack2(x3.astype(BF16)),
                          preferred_element_type=F32)

    v3 = pairs(v_ref)

    lhs = jnp.concatenate([pairs(at_ref), pairs(rt_ref)], axis=1)
    bt3 = pairs(bt_ref)
    kt3 = pairs(kt_ref)
    rhs = jnp.concatenate([stack2(bt3), stack2(kt3)], axis=1)
    g = jnp.einsum("ptk,psk->pts", lhs, rhs, preferred_element_type=F32)
    trow = jax.lax.broadcasted_iota(jnp.int32, (1, T, LANES), 1)
    scol = jax.lax.broadcasted_iota(jnp.int32, (1, T, LANES), 2) % RW_HEAD_DIM
    strict = trow > scol
    incl = trow >= scol
    n_ab = jnp.where(strict, g[:, :T, :LANES], 0.0)
    a_ak = jnp.where(strict, g[:, :T, LANES:], 0.0)
    a_rb = jnp.where(incl, g[:, T:, :LANES], 0.0)
    a_rk = jnp.where(incl, g[:, T:, LANES:], 0.0)

    h0 = h_ref[...]
    ah = jnp.einsum("ptk,pkv->ptv", lhs, h0.astype(BF16), preferred_element_type=F32)
    x = ah[:, :T] + pairmul(a_ak, v3)
    pm = n_ab
    for lvl in range(SOLVE_LEVELS):
        x = x + pairmul(pm, x)
        if lvl + 1 < SOLVE_LEVELS:
            pm = pairmul(pm, pm)
    u = x
    o3 = ah[:, T:] + pairmul(a_rb, u) + pairmul(a_rk, v3)

    upd_l = jnp.swapaxes(jnp.concatenate([bt3, kt3], axis=1).astype(F32), 1, 2)
    upd_r = jnp.concatenate([u.astype(BF16), v3], axis=1)
    upd = jnp.einsum("pks,psv->pkv", upd_l.astype(BF16), upd_r, preferred_element_type=F32)
    etot3 = _to_pairs(etot_ref[...].reshape(nb, W), nb)
    gcol = jnp.swapaxes(jnp.broadcast_to(etot3, (NP, LANES, LANES)), 1, 2)
    li = jax.lax.broadcasted_iota(jnp.int32, (1, LANES, LANES), 1) // RW_HEAD_DIM
    lj = jax.lax.broadcasted_iota(jnp.int32, (1, LANES, LANES), 2) // RW_HEAD_DIM
    h_ref[...] = gcol * (h0 + jnp.where(li == lj, upd, 0.0))

    inv_n = 1.0 / RW_HEAD_DIM
    oc = o3 - _seg_sum(o3) * inv_n
    ov = _seg_sum(oc * oc) * inv_n
    on = _from_pairs(oc * jax.lax.rsqrt(ov + GN_EPS), nb) * gg_ref[...] + gb_ref[...]
    bonus = bonus_ref[...].reshape(nb * T, W).astype(F32)
    zg = zg_ref[...].reshape(nb * T, W).astype(F32)
    yb_ref[...] = ((on + bonus) * zg).reshape(nb, T, W).astype(yb_ref.dtype)


def _rwkv(staged, etot, gn_g, gn_b, nb):
    bsz, seq, _ = staged[0].shape
    row = pl.BlockSpec((1, RW_WIDTH), lambda b, c: (0, 0))
    blk = pl.BlockSpec((nb, RW_T, RW_WIDTH), lambda b, c: (b, c, 0))
    return pl.pallas_call(
        functools.partial(_rwkv_kernel, nb=nb),
        grid=(bsz // nb, seq // RW_T),
        in_specs=[blk] * 7 + [pl.BlockSpec((nb, 1, 1, RW_WIDTH), lambda b, c: (b, c, 0, 0)), row, row],
        out_specs=blk,
        out_shape=jax.ShapeDtypeStruct((bsz, seq, RW_WIDTH), BF16),
        scratch_shapes=[
            pltpu.VMEM((nb * PAIRS, LANES, LANES), F32),
        ],
        compiler_params=pltpu.CompilerParams(
            dimension_semantics=("parallel", "arbitrary"), vmem_limit_bytes=VMEM_LIMIT),
        name="rwkv7",
    )(*staged, etot, gn_g, gn_b)


def _out_kernel(x_ref, ya_ref, yb_ref, ga_ref, gb_ref, wa_ref, wb_ref, wo_ref, g_ref, o_ref):
    ma = _bdot(ya_ref[...], wa_ref[...])
    mb = _bdot(yb_ref[...], wb_ref[...])
    merged = ga_ref[...].astype(F32) * ma + gb_ref[...].astype(F32) * mb
    y = _bdot(merged, wo_ref[...])
    yn = y * jax.lax.rsqrt(jnp.mean(y * y, axis=-1, keepdims=True) + RMS_EPS) * g_ref[...]
    o_ref[...] = x_ref[...] + yn


def _out(x2, ya, yb, gates, wa, wb, wo, g, tm):
    n = x2.shape[0]
    tok = lambda j: pl.BlockSpec((tm, D_MODEL), lambda i: (i, j))
    wspec = pl.BlockSpec((D_MODEL, D_MODEL), lambda i: (0, 0))
    return pl.pallas_call(
        _out_kernel,
        grid=(n // tm,),
        in_specs=[tok(0), tok(0), tok(0), tok(0), tok(1), wspec, wspec, wspec,
                  pl.BlockSpec((1, D_MODEL), lambda i: (0, 0))],
        out_specs=tok(0),
        out_shape=jax.ShapeDtypeStruct((n, D_MODEL), F32),
        compiler_params=pltpu.CompilerParams(
            dimension_semantics=("parallel",), vmem_limit_bytes=VMEM_LIMIT),
        name="merge_out",
    )(x2, ya, yb, gates, gates, wa, wb, wo, g)


def _layer(x, norm_pre_g, w_in, gm_ln_g, gm_ln_b, gm_w_s, gm_b_s, rw_mu, rw_w0, rw_decay_up, rw_a0,
           rw_iclr_up, rw_k_k, rw_k_a, rw_r_k, rw_gn_g, rw_gn_b, w_branch_a, w_branch_b, w_out,
           norm_post_g):
    bsz, seq, d = x.shape
    n = bsz * seq
    x2 = x.reshape(n, d)
    row = lambda a: a.reshape(1, -1)

    zeros = jnp.zeros_like(rw_decay_up)
    lr_w = jnp.concatenate([jnp.concatenate([rw_decay_up, zeros], axis=1),
                            jnp.concatenate([zeros, rw_iclr_up], axis=1)], axis=0).astype(BF16)
    def regroup(a):
        main = a[:, :4 * RW_WIDTH].reshape(a.shape[0], 4, PAIRS, LANES)
        return jnp.concatenate([jnp.swapaxes(main, 1, 2).reshape(a.shape[0], 4 * RW_WIDTH),
                                a[:, 4 * RW_WIDTH:]], axis=1)

    w_rw = regroup(w_in[:, GM_COLS:GM_COLS + RW_COLS])
    w_front = jnp.concatenate([w_rw, w_in[:, :GM_COLS], w_in[:, GM_COLS + RW_COLS:]], axis=1).astype(BF16)
    ya, gates, *staged, etot = _front(
        x2, seq, row(norm_pre_g), w_front, row(gm_ln_g), row(gm_ln_b), gm_w_s,
        gm_b_s[:, :, None], regroup(row(rw_mu)), row(rw_w0), row(rw_a0), lr_w, row(rw_k_k), row(rw_k_a),
        row(rw_r_k), tm=min(FRONT_TM, seq))
    yb = _rwkv([a.reshape(bsz, seq, RW_WIDTH) for a in staged],
               etot.reshape(bsz, seq // RW_T, 1, RW_WIDTH), row(rw_gn_g), row(rw_gn_b),
               nb=RW_NB if bsz % RW_NB == 0 else 1)

    out = _out(x2, ya, yb.reshape(n, RW_WIDTH), gates, w_branch_a.astype(BF16), w_branch_b.astype(BF16),
               w_out.astype(BF16), row(norm_post_g), tm=min(512, n))
    return out.reshape(bsz, seq, d)


def kernel(x, norm_pre_g, w_in, gm_ln_g, gm_ln_b, gm_w_s, gm_b_s, rw_mu, rw_w0, rw_decay_up, rw_a0,
           rw_iclr_up, rw_k_k, rw_k_a, rw_r_k, rw_gn_g, rw_gn_b, w_branch_a, w_branch_b, w_out,
           norm_post_g):
    depth = norm_pre_g.shape[0]
    for l in range(depth):
        x = _layer(x, norm_pre_g[l], w_in[l], gm_ln_g[l], gm_ln_b[l], gm_w_s[l], gm_b_s[l], rw_mu[l],
                   rw_w0[l], rw_decay_up[l], rw_a0[l], rw_iclr_up[l], rw_k_k[l], rw_k_a[l],
                   rw_r_k[l], rw_gn_g[l], rw_gn_b[l], w_branch_a[l], w_branch_b[l], w_out[l],
                   norm_post_g[l])
    return x
```
